```python
import math
import jax, jax.numpy as jnp
from jax import lax
import numpy as np

D_MODEL = 2048
BATCH = 4
SEQ = 8192
DEPTH = 1

S5_WIDTH = D_MODEL // 4
S5_GROUP = 16
S5_GROUPS = S5_WIDTH // S5_GROUP
S5_STATE = 64
N_DIR = 2
DT_MIN = 1e-3
DT_MAX = 1e-1
FNET_WIDTH = D_MODEL - S5_WIDTH
FNET_GROUP = 256
FNET_GROUPS = FNET_WIDTH // FNET_GROUP
IN_WIDTH = S5_WIDTH + FNET_WIDTH + 2 * D_MODEL
D_FF = -(-8 * D_MODEL // (3 * 256)) * 256
N_MOD = 6
EPS = 1e-6

kernel_name = "hybrid_s5_fnet_gated_encoder_block"


def rms_norm(x, g):
    xf = x.astype(jnp.float32)
    y = xf * lax.rsqrt(jnp.mean(xf * xf, axis=-1, keepdims=True) + EPS)
    return (y * g.astype(jnp.float32)).astype(x.dtype)


def modulate(h, shift, scale):
    return h * (1.0 + scale[:, None, :]) + shift[:, None, :]


def _complex_combine(left, right):
    a1r, a1i, b1r, b1i = left
    a2r, a2i, b2r, b2i = right
    ar = a2r * a1r - a2i * a1i
    ai = a2r * a1i + a2i * a1r
    br = a2r * b1r - a2i * b1i + b2r
    bi = a2r * b1i + a2i * b1r + b2i
    return ar, ai, br, bi


def s5_direction(u, lam_re, lam_im, log_step, b_re, b_im, c_re, c_im, reverse):
    dt = jnp.exp(log_step)[:, None]
    mag = jnp.exp(lam_re * dt)
    ang = lam_im * dt
    lb_re = mag * jnp.cos(ang)
    lb_im = mag * jnp.sin(ang)
    den = lam_re * lam_re + lam_im * lam_im
    num_re = lb_re - 1.0
    num_im = lb_im
    coef_re = (num_re * lam_re + num_im * lam_im) / den
    coef_im = (num_im * lam_re - num_re * lam_im) / den
    bb_re = coef_re[..., None] * b_re - coef_im[..., None] * b_im
    bb_im = coef_re[..., None] * b_im + coef_im[..., None] * b_re
    bu_re = jnp.einsum('bsgh,gnh->bsgn', u, bb_re)
    bu_im = jnp.einsum('bsgh,gnh->bsgn', u, bb_im)
    a_re = jnp.broadcast_to(lb_re, bu_re.shape)
    a_im = jnp.broadcast_to(lb_im, bu_im.shape)
    _, _, st_re, st_im = lax.associative_scan(
        _complex_combine, (a_re, a_im, bu_re, bu_im), axis=1, reverse=reverse)
    return (jnp.einsum('bsgn,ghn->bsgh', st_re, c_re)
            - jnp.einsum('bsgn,ghn->bsgh', st_im, c_im))


def s5_branch(u_in, lam_re, lam_im, log_step, b_re, b_im, c_re, c_im, d_skip, w_glu):
    bsz, seq, _ = u_in.shape
    u = u_in.astype(jnp.float32).reshape(bsz, seq, S5_GROUPS, S5_GROUP)
    f32 = lambda t: t.astype(jnp.float32)
    y = d_skip.astype(jnp.float32).reshape(S5_GROUPS, S5_GROUP) * u
    for d in range(N_DIR):
        y = y + s5_direction(u, f32(lam_re[d]), f32(lam_im[d]), f32(log_step[d]),
                             f32(b_re[d]), f32(b_im[d]), f32(c_re[d]), f32(c_im[d]),
                             reverse=(d == 1))
    y = jax.nn.gelu(y.reshape(bsz, seq, S5_WIDTH)).astype(u_in.dtype)
    val, gate = jnp.split(y @ w_glu, 2, axis=-1)
    return val * jax.nn.sigmoid(gate)


def fnet_branch(u_in):
    bsz, seq, _ = u_in.shape
    u = u_in.astype(jnp.float32).reshape(bsz, seq, FNET_GROUPS, FNET_GROUP)
    z = jnp.fft.fft2(u, axes=(1, 3), norm='ortho').real
    return z.reshape(bsz, seq, FNET_WIDTH).astype(u_in.dtype)


def setup_inputs(seed: int = 0) -> dict:
    key = jax.random.key(seed)
    ks = jax.random.split(key, 24)
    nrm = jax.random.normal

    def dense(k, shape, fan_in):
        return nrm(k, shape, jnp.float32) * (fan_in ** -0.5)

    L = DEPTH
    x = nrm(ks[0], (BATCH, SEQ, D_MODEL), jnp.float32)
    c = nrm(ks[1], (BATCH, D_MODEL), jnp.float32)
    w_ada = dense(ks[2], (L, D_MODEL, N_MOD * D_MODEL), D_MODEL)
    b_ada = 0.01 * nrm(ks[3], (L, N_MOD * D_MODEL), jnp.float32)
    norm_mix = 1.0 + 0.01 * nrm(ks[4], (L, D_MODEL), jnp.float32)
    w_in = dense(ks[5], (L, D_MODEL, IN_WIDTH), D_MODEL)
    sshape = (L, N_DIR, S5_GROUPS, S5_STATE)
    s5_lambda_re = -0.5 + 0.01 * nrm(ks[6], sshape, jnp.float32)
    s5_lambda_im = (jnp.pi * jnp.arange(S5_STATE, dtype=jnp.float32)
                    + 0.01 * nrm(ks[7], sshape, jnp.float32))
    s5_log_step = jax.random.uniform(ks[8], (L, N_DIR, S5_GROUPS), jnp.float32,
                                     math.log(DT_MIN), math.log(DT_MAX))
    bshape = (L, N_DIR, S5_GROUPS, S5_STATE, S5_GROUP)
    s5_b_re = dense(ks[9], bshape, 2 * S5_GROUP)
    s5_b_im = dense(ks[10], bshape, 2 * S5_GROUP)
    cshape = (L, N_DIR, S5_GROUPS, S5_GROUP, S5_STATE)
    s5_c_re = dense(ks[11], cshape, 2 * S5_STATE)
    s5_c_im = dense(ks[12], cshape, 2 * S5_STATE)
    s5_d = nrm(ks[13], (L, S5_WIDTH), jnp.float32)
    w_s5_glu = dense(ks[14], (L, S5_WIDTH, 2 * S5_WIDTH), S5_WIDTH)
    w_branch_s5 = dense(ks[15], (L, S5_WIDTH, D_MODEL), S5_WIDTH)
    w_branch_fnet = dense(ks[16], (L, FNET_WIDTH, D_MODEL), FNET_WIDTH)
    w_out = dense(ks[17], (L, D_MODEL, D_MODEL), D_MODEL)
    norm_ffn = 1.0 + 0.01 * nrm(ks[18], (L, D_MODEL), jnp.float32)
    w_ffn_in = dense(ks[19], (L, D_MODEL, 2 * D_FF), D_MODEL)
    w_ffn_out = dense(ks[20], (L, D_FF, D_MODEL), D_FF)
    norm_final = 1.0 + 0.01 * nrm(ks[21], (D_MODEL,), jnp.float32)
    return {"x": x, "c": c, "w_ada": w_ada, "b_ada": b_ada, "norm_mix": norm_mix,
            "w_in": w_in, "s5_lambda_re": s5_lambda_re, "s5_lambda_im": s5_lambda_im,
            "s5_log_step": s5_log_step, "s5_b_re": s5_b_re, "s5_b_im": s5_b_im,
            "s5_c_re": s5_c_re, "s5_c_im": s5_c_im, "s5_d": s5_d, "w_s5_glu": w_s5_glu,
            "w_branch_s5": w_branch_s5, "w_branch_fnet": w_branch_fnet, "w_out": w_out,
            "norm_ffn": norm_ffn, "w_ffn_in": w_ffn_in, "w_ffn_out": w_ffn_out,
            "norm_final": norm_final}


def reference(x, c, w_ada, b_ada, norm_mix, w_in, s5_lambda_re, s5_lambda_im, s5_log_step,
              s5_b_re, s5_b_im, s5_c_re, s5_c_im, s5_d, w_s5_glu, w_branch_s5, w_branch_fnet,
              w_out, norm_ffn, w_ffn_in, w_ffn_out, norm_final):
    c_act = jax.nn.silu(c)
    for l in range(DEPTH):
        mod = c_act @ w_ada[l] + b_ada[l]
        sh_m, sc_m, g_m, sh_f, sc_f, g_f = jnp.split(mod, N_MOD, axis=-1)

        h = modulate(rms_norm(x, norm_mix[l]), sh_m, sc_m)
        proj = h @ w_in[l]
        o1 = S5_WIDTH
        o2 = o1 + FNET_WIDTH
        o3 = o2 + D_MODEL
        y_s5 = s5_branch(proj[..., :o1], s5_lambda_re[l], s5_lambda_im[l], s5_log_step[l],
                         s5_b_re[l], s5_b_im[l], s5_c_re[l], s5_c_im[l], s5_d[l], w_s5_glu[l])
        y_fn = fnet_branch(proj[..., o1:o2])
        gate_s5 = jax.nn.sigmoid(proj[..., o2:o3])
        gate_fn = jax.nn.sigmoid(proj[..., o3:])
        merged = gate_s5 * (y_s5 @ w_branch_s5[l]) + gate_fn * (y_fn @ w_branch_fnet[l])
        x = x + g_m[:, None, :] * (merged @ w_out[l])

        h2 = modulate(rms_norm(x, norm_ffn[l]), sh_f, sc_f)
        a, b = jnp.split(h2 @ w_ffn_in[l], 2, axis=-1)
        x = x + g_f[:, None, :] * ((jax.nn.silu(a) * b) @ w_ffn_out[l])
    return rms_norm(x, norm_final)
```

```python
import functools
import math

import jax
import jax.numpy as jnp
import numpy as np
from jax import lax
from jax.experimental import pallas as pl
from jax.experimental.pallas import tpu as pltpu

F32 = jnp.float32
BF16 = jnp.bfloat16

S5_GROUP = 16
S5_STATE = 64
FNET_GROUP = 256
N_MOD = 6
EPS = 1e-6

LANES = 128
SUBLANES = 8
BF16_ROWS = 16
VMEM_LIMIT_BYTES = 60 * 1024 * 1024

CHUNK = LANES
FFT_N1 = 128
FFT_K1_PER_DOT = SUBLANES

ROW_TILE = 512
FFN_TILE = 512
ADA_TILE = 1024
FFT1_TILE = 4096


def _params(*semantics):
    return pltpu.CompilerParams(dimension_semantics=semantics, vmem_limit_bytes=VMEM_LIMIT_BYTES)


def _const_spec(shape):
    zeros = (0,) * len(shape)
    return pl.BlockSpec(shape, lambda *_: zeros, pipeline_mode=pl.Buffered(1))


def _dot(a, b):
    return jnp.dot(a, b, preferred_element_type=F32)


def _dot_nt(a, b, precision=None):
    return lax.dot_general(a, b, (((1,), (1,)), ((), ())), precision=precision,
                           preferred_element_type=F32)


def _rms_modulate(x, gain, scale, shift):
    ms = jnp.mean(x * x, axis=-1, keepdims=True)
    y = x * lax.rsqrt(ms + EPS)
    return (y * gain) * (1.0 + scale) + shift


def _ada_kernel(ct_ref, w_ref, b_ref, o_ref):
    d, nb = ct_ref.shape
    tn = o_ref.shape[1]

    def body(k, accs):
        k0 = pl.multiple_of(k * SUBLANES, SUBLANES)
        w = w_ref[pl.ds(k0, SUBLANES), :]
        cs = ct_ref[pl.ds(k0, SUBLANES), :]
        cs = cs * jax.nn.sigmoid(cs)
        return tuple(acc + w * cs[:, b:b + 1] for b, acc in enumerate(accs))

    init = tuple(jnp.zeros((SUBLANES, tn), F32) for _ in range(nb))
    accs = lax.fori_loop(0, d // SUBLANES, body, init, unroll=2)
    for b, acc in enumerate(accs):
        o_ref[b:b + 1, :] = jnp.sum(acc, axis=0, keepdims=True) + b_ref[...]


def _ada(ct, w, bias):
    d, nb = ct.shape
    n = w.shape[1]
    return pl.pallas_call(
        _ada_kernel,
        grid=(n // ADA_TILE,),
        in_specs=[pl.BlockSpec((d, nb), lambda j: (0, 0)),
                  pl.BlockSpec((d, ADA_TILE), lambda j: (0, j)),
                  pl.BlockSpec((1, ADA_TILE), lambda j: (0, j))],
        out_specs=pl.BlockSpec((nb, ADA_TILE), lambda j: (0, j)),
        out_shape=jax.ShapeDtypeStruct((nb, n), F32),
        compiler_params=_params("arbitrary"),
        name="ada",
    )(ct, w, bias.reshape(1, n))


def _inproj_kernel(x_ref, g_ref, sc_ref, sh_ref, ws5t_ref, wfn_ref, wg_ref, dft_ref,
                   us5_ref, v_ref, gs_ref, gf_ref):
    tm, d = x_ref.shape
    hb = _rms_modulate(x_ref[...], g_ref[...], sc_ref[...], sh_ref[...]).astype(BF16)

    ut = _dot_nt(ws5t_ref[...], hb)
    for ci in range(tm // CHUNK):
        us5_ref[ci] = ut[:, ci * CHUNK:(ci + 1) * CHUNK].astype(BF16)

    wfn = wfn_ref.shape[1]
    for g in range(wfn // FNET_GROUP):
        cols = slice(g * FNET_GROUP, (g + 1) * FNET_GROUP)
        uf = _dot(hb, wfn_ref[:, cols]).astype(BF16)
        pq = _dot(uf, dft_ref[...])
        v_ref[0, :, cols] = pq[:, :FNET_GROUP].astype(BF16)
        v_ref[1, :, cols] = pq[:, FNET_GROUP:].astype(BF16)

    nb = 512
    for n0 in range(0, 2 * d, nb):
        s = jax.nn.sigmoid(_dot(hb, wg_ref[:, n0:n0 + nb])).astype(BF16)
        if n0 < d:
            gs_ref[:, n0:n0 + nb] = s
        else:
            gf_ref[:, n0 - d:n0 - d + nb] = s


def _inproj(x2d, gain, scale, shift, ws5t, wfn, wg, dft, batch, seq):
    t, d = x2d.shape
    tm = ROW_TILE
    bps = seq // tm
    w5 = ws5t.shape[0]
    wf = wfn.shape[1]
    per_batch = pl.BlockSpec((None, 1, d), lambda i: (i // bps, 0, 0))
    return pl.pallas_call(
        _inproj_kernel,
        grid=(t // tm,),
        in_specs=[pl.BlockSpec((tm, d), lambda i: (i, 0)),
                  _const_spec((1, d)), per_batch, per_batch,
                  _const_spec(ws5t.shape), _const_spec(wfn.shape), _const_spec(wg.shape),
                  _const_spec(dft.shape)],
        out_specs=[pl.BlockSpec((tm // CHUNK, w5, CHUNK), lambda i: (i, 0, 0)),
                   pl.BlockSpec((None, 2, tm, wf), lambda i: (i // bps, 0, i % bps, 0)),
                   pl.BlockSpec((tm, d), lambda i: (i, 0)),
                   pl.BlockSpec((tm, d), lambda i: (i, 0))],
        out_shape=[jax.ShapeDtypeStruct((t // CHUNK, w5, CHUNK), BF16),
                   jax.ShapeDtypeStruct((batch, 2, seq, wf), BF16),
                   jax.ShapeDtypeStruct((t, d), BF16),
                   jax.ShapeDtypeStruct((t, d), BF16)],
        compiler_params=_params("arbitrary"),
        name="inproj",
    )(x2d, gain, scale, shift, ws5t, wfn, wg, dft)


def _s5_kernel(x_ref, lre_ref, lim_ref, ls_ref, btr_ref, bti_ref, cr_ref, ci_ref, dcol_ref,
               y_ref, v_scr, w_scr, win_scr, wout_scr, *, chunks_per_seq):
    nc = x_ref.shape[0]
    ns = 2 * S5_STATE
    hg = S5_GROUP
    lane = lax.broadcasted_iota(jnp.int32, (1, ns), 1)
    fwd = lane < S5_STATE

    lam_re = lre_ref[...]
    lam_im = lim_ref[...]
    dt = jnp.exp(ls_ref[...])
    lr = lam_re * dt
    li = lam_im * dt

    def powers(e):
        mag = jnp.exp(e * lr)
        return mag * jnp.cos(e * li), mag * jnp.sin(e * li)

    lb_re, lb_im = powers(jnp.ones((1, ns), F32))
    den = lam_re * lam_re + lam_im * lam_im
    num_re = lb_re - 1.0
    num_im = lb_im
    coef_re = (num_re * lam_re + num_im * lam_im) / den
    coef_im = (num_im * lam_re - num_re * lam_im) / den
    btr = btr_ref[...]
    bti = bti_ref[...]
    bbr = coef_re * btr - coef_im * bti
    bbi = coef_re * bti + coef_im * btr
    cr = cr_ref[...]
    ci = ci_ref[...]

    qr = jnp.concatenate([cr[h:h + 1] * bbr - ci[h:h + 1] * bbi for h in range(hg)], axis=0)
    qi = jnp.concatenate([cr[h:h + 1] * bbi + ci[h:h + 1] * bbr for h in range(hg)], axis=0)

    m_idx = lax.broadcasted_iota(jnp.int32, (2 * CHUNK, ns), 0)
    pr, pi = powers(jnp.abs(m_idx - CHUNK).astype(F32))
    active = (m_idx >= jnp.where(fwd, CHUNK, 0)) & (m_idx <= jnp.where(fwd, 2 * CHUNK, CHUNK))
    pr = jnp.where(active, pr, 0.0)
    pi = jnp.where(active, pi, 0.0)
    hi = lax.Precision.HIGHEST
    v = _dot_nt(qr, pr, hi) - _dot_nt(qi, pi, hi)
    centre = lax.broadcasted_iota(jnp.int32, (1, 2 * CHUNK), 1) == CHUNK
    v_scr[...] = v + jnp.where(centre, dcol_ref[...], 0.0)

    def expand(hp, carry):
        for hl in range(2):
            for hq in range(hg):
                r = (2 * hp + hl) * hg + hq
                row = jnp.broadcast_to(v_scr[pl.ds(r, 1), :], (CHUNK, 2 * CHUNK))
                blk = pltpu.roll(row, CHUNK, 1, stride=1, stride_axis=0)
                w_scr[hp, hq * CHUNK:(hq + 1) * CHUNK, hl * CHUNK:(hl + 1) * CHUNK] = (
                    blk[:, :CHUNK].astype(BF16))
        return carry

    lax.fori_loop(0, hg // 2, expand, 0)

    j_idx = lax.broadcasted_iota(jnp.int32, (CHUNK, ns), 0)
    pin_r, pin_i = powers(jnp.where(fwd, CHUNK - 1 - j_idx, j_idx).astype(F32))
    for hq in range(hg):
        rows = slice(hq * CHUNK, (hq + 1) * CHUNK)
        br = bbr[hq:hq + 1]
        bi = bbi[hq:hq + 1]
        win_scr[rows, :ns] = (pin_r * br - pin_i * bi).astype(BF16)
        win_scr[rows, ns:] = (pin_r * bi + pin_i * br).astype(BF16)

    po_r, po_i = powers(jnp.where(fwd, j_idx + 1, CHUNK - j_idx).astype(F32))
    for h in range(hg):
        rows = slice(h * CHUNK, (h + 1) * CHUNK)
        c_r = cr[h:h + 1]
        c_i = ci[h:h + 1]
        wout_scr[rows, :ns] = (c_r * po_r - c_i * po_i).astype(BF16)
        wout_scr[rows, ns:] = (-(c_r * po_i + c_i * po_r)).astype(BF16)

    x = x_ref[...]
    z = _dot(x, win_scr[...])
    zr = z[:, :ns]
    zi = z[:, ns:]

    pos = lax.broadcasted_iota(jnp.int32, (nc, ns), 0) & (chunks_per_seq - 1)

    def in_seq(dist):
        return ((pos >= jnp.where(fwd, dist, 0))
                & (pos < jnp.where(fwd, chunks_per_seq, chunks_per_seq - dist)))

    def neighbour(a, dist):
        return jnp.where(fwd, pltpu.roll(a, dist, 0), pltpu.roll(a, nc - dist, 0))

    dist = 1
    while dist < chunks_per_seq:
        ar, ai = powers(jnp.full((1, ns), float(CHUNK * dist), F32))
        nr = neighbour(zr, dist)
        ni = neighbour(zi, dist)
        ok = in_seq(dist)
        zr, zi = (zr + jnp.where(ok, ar * nr - ai * ni, 0.0),
                  zi + jnp.where(ok, ar * ni + ai * nr, 0.0))
        dist *= 2
    ok = in_seq(1)
    sr = jnp.where(ok, neighbour(zr, 1), 0.0)
    si = jnp.where(ok, neighbour(zi, 1), 0.0)
    s_in = jnp.concatenate([sr, si], axis=1).astype(BF16)
    y_state = _dot_nt(s_in, wout_scr[...])

    wide = 2 * CHUNK
    for hp in range(hg // 2):
        cols = slice(hp * wide, (hp + 1) * wide)
        y_ref[:, cols] = _dot(x, w_scr[hp]) + y_state[:, cols]


def _s5(us5, lre, lim, ls, btr, bti, cr, ci, dcol, chunks_per_seq):
    nc, width = us5.shape
    groups = lre.shape[0]
    gw = S5_GROUP * CHUNK
    ns = 2 * S5_STATE
    assert chunks_per_seq & (chunks_per_seq - 1) == 0
    row = pl.BlockSpec((None, 1, ns), lambda g: (g, 0, 0))
    mat = pl.BlockSpec((None, S5_GROUP, ns), lambda g: (g, 0, 0))
    return pl.pallas_call(
        functools.partial(_s5_kernel, chunks_per_seq=chunks_per_seq),
        grid=(groups,),
        in_specs=[pl.BlockSpec((nc, gw), lambda g: (0, g)),
                  row, row, row, mat, mat, mat, mat,
                  pl.BlockSpec((None, S5_GROUP * S5_GROUP, 1), lambda g: (g, 0, 0))],
        out_specs=pl.BlockSpec((nc, gw), lambda g: (0, g)),
        out_shape=jax.ShapeDtypeStruct((nc, width), F32),
        scratch_shapes=[pltpu.VMEM((S5_GROUP * S5_GROUP, 2 * CHUNK), F32),
                        pltpu.VMEM((S5_GROUP // 2, gw, 2 * CHUNK), BF16),
                        pltpu.VMEM((gw, 2 * ns), BF16),
                        pltpu.VMEM((gw, 2 * ns), BF16)],
        compiler_params=_params("arbitrary"),
        name="s5",
    )(us5, lre, lim, ls, btr, bti, cr, ci, dcol)


def _fft1_kernel(f_ref, v_ref, a_ref):
    a_ref[...] = _dot(f_ref[...], v_ref[...]).astype(BF16)


def _fft1(f1, v3):
    batch, rows, cols = v3.shape
    blk = pl.BlockSpec((None, rows, FFT1_TILE), lambda b, j: (b, 0, j))
    return pl.pallas_call(
        _fft1_kernel,
        grid=(batch, cols // FFT1_TILE),
        in_specs=[_const_spec(f1.shape), blk],
        out_specs=blk,
        out_shape=jax.ShapeDtypeStruct(v3.shape, BF16),
        compiler_params=_params("arbitrary", "arbitrary"),
        name="fft1",
    )(f1, v3)


def _fft2_kernel(m_ref, a_ref, o_ref):
    n2, kb, c = o_ref.shape
    per = FFT_K1_PER_DOT
    parts = []
    for half in range(kb // per):
        ks = slice(half * per, (half + 1) * per)
        rhs = jnp.concatenate([a_ref[0, ks].reshape(per * n2, c),
                               a_ref[1, ks].reshape(per * n2, c)], axis=0)
        parts.append(_dot(m_ref[half], rhs).reshape(n2, per, c))
    o_ref[...] = jnp.concatenate(parts, axis=1).astype(BF16)


def _fft2(m2, a5):
    batch, _, n1, n2, c = a5.shape
    kb = BF16_ROWS
    dots = kb // FFT_K1_PER_DOT
    return pl.pallas_call(
        _fft2_kernel,
        grid=(n1 // kb, batch),
        in_specs=[pl.BlockSpec((dots,) + m2.shape[1:], lambda k, b: (k, 0, 0)),
                  pl.BlockSpec((None, 2, kb, n2, c), lambda k, b: (b, 0, k, 0, 0))],
        out_specs=pl.BlockSpec((None, n2, kb, c), lambda k, b: (b, 0, k, 0)),
        out_shape=jax.ShapeDtypeStruct((batch, n2, n1, c), BF16),
        compiler_params=_params("arbitrary", "arbitrary"),
        name="fft2",
    )(m2, a5)


def _chan_dft_table():
    n = FNET_GROUP
    idx = np.arange(n)
    ang = 2.0 * np.pi * ((idx[:, None] * idx[None, :]) % n) / n
    return (np.concatenate([np.cos(ang), -np.sin(ang)], axis=1) / math.sqrt(n)).astype(np.float32)


def _fft1_table(n1, seq):
    idx = np.arange(n1)
    ang = 2.0 * np.pi * ((idx[:, None] * idx[None, :]) % n1) / n1
    c, s = np.cos(ang), np.sin(ang)
    return (np.block([[c, s], [-s, c]]) / math.sqrt(seq)).astype(np.float32)


def _fft2_table(n1, n2):
    per = FFT_K1_PER_DOT
    n = n1 * n2
    k1 = np.arange(n1)[:, None, None]
    k2 = np.arange(n2)[None, :, None]
    nn = np.arange(n2)[None, None, :]
    ang = 2.0 * np.pi * ((nn * (k1 + n1 * k2)) % n) / n
    g = jnp.asarray(np.stack([np.cos(ang), np.sin(ang)], axis=2).astype(np.float32))
    g = g.reshape(n1 // per, per, n2, 2, n2).transpose(0, 2, 1, 3, 4)
    eye = jnp.eye(per, dtype=F32)
    m = g[:, :, :, :, None, :] * eye[None, None, :, None, :, None]
    return m.reshape(n1 // per, n2 * per, 2 * per * n2).astype(BF16)


def _merge_kernel(x_ref, ys5_ref, yfn_ref, gs_ref, gf_ref, gm_ref, g2_ref, sc_ref, sh_ref,
                  wglu_ref, wbs_ref, wbf_ref, wo_ref, x1_ref, h2_ref, m_scr):
    tm, d = x_ref.shape
    w5 = wbs_ref.shape[0]
    parts = [jax.nn.gelu(ys5_ref[ci].T, approximate=True).astype(BF16)
             for ci in range(tm // CHUNK)]
    ge = jnp.concatenate(parts, axis=0)
    z = _dot(ge, wglu_ref[...])
    ys = (z[:, :w5] * jax.nn.sigmoid(z[:, w5:])).astype(BF16)
    yfn = yfn_ref[...]
    nb = 512
    for n0 in range(0, d, nb):
        cols = slice(n0, n0 + nb)
        m_scr[:, cols] = (gs_ref[:, cols].astype(F32) * _dot(ys, wbs_ref[:, cols])
                          + gf_ref[:, cols].astype(F32) * _dot(yfn, wbf_ref[:, cols])).astype(BF16)
    merged = m_scr[...]
    for n0 in range(0, d, nb):
        cols = slice(n0, n0 + nb)
        x1_ref[:, cols] = x_ref[:, cols] + gm_ref[:, cols] * _dot(merged, wo_ref[:, cols])
    h2_ref[...] = _rms_modulate(x1_ref[...], g2_ref[...], sc_ref[...], sh_ref[...]).astype(BF16)


def _merge(x2d, ys5, yfn, gs, gf, gm, g2, scale, shift, wglu, wbs, wbf, wo, seq):
    t, d = x2d.shape
    tm = ROW_TILE
    bps = seq // tm
    w5 = ys5.shape[1]
    wf = yfn.shape[1]
    rows = lambda width: pl.BlockSpec((tm, width), lambda i: (i, 0))
    per_batch = pl.BlockSpec((None, 1, d), lambda i: (i // bps, 0, 0))
    return pl.pallas_call(
        _merge_kernel,
        grid=(t // tm,),
        in_specs=[rows(d),
                  pl.BlockSpec((tm // CHUNK, w5, CHUNK), lambda i: (i, 0, 0)),
                  rows(wf), rows(d), rows(d),
                  per_batch, _const_spec((1, d)), per_batch, per_batch,
                  _const_spec(wglu.shape), _const_spec(wbs.shape), _const_spec(wbf.shape),
                  _const_spec(wo.shape)],
        out_specs=[rows(d), rows(d)],
        out_shape=[jax.ShapeDtypeStruct((t, d), F32), jax.ShapeDtypeStruct((t, d), BF16)],
        scratch_shapes=[pltpu.VMEM((tm, d), BF16)],
        compiler_params=_params("arbitrary"),
        name="merge",
    )(x2d, ys5, yfn, gs, gf, gm, g2, scale, shift, wglu, wbs, wbf, wo)


def _ffn_kernel(h_ref, wa_ref, wb_ref, wo_ref, x1_ref, gate_ref, gn_ref, o_ref):
    f = pl.program_id(1)

    @pl.when(f == 0)
    def _():
        o_ref[...] = jnp.zeros_like(o_ref)

    h = h_ref[...]
    a = _dot(h, wa_ref[...])
    b = _dot(h, wb_ref[...])
    act = (a * jax.nn.sigmoid(a) * b).astype(BF16)
    o_ref[...] += _dot(act, wo_ref[...])

    @pl.when(f == pl.num_programs(1) - 1)
    def _():
        x2 = x1_ref[...] + gate_ref[...] * o_ref[...]
        ms = jnp.mean(x2 * x2, axis=-1, keepdims=True)
        o_ref[...] = (x2 * lax.rsqrt(ms + EPS)) * gn_ref[...]


def _ffn(h2, w_in, w_out, x1, gate, gn, seq):
    t, d = h2.shape
    ff = w_out.shape[0]
    tm, tf = ROW_TILE, FFN_TILE
    bps = seq // tm
    nf = ff // tf
    return pl.pallas_call(
        _ffn_kernel,
        grid=(t // tm, nf),
        in_specs=[pl.BlockSpec((tm, d), lambda i, f: (i, 0)),
                  pl.BlockSpec((d, tf), lambda i, f: (0, f)),
                  pl.BlockSpec((d, tf), lambda i, f: (0, f + nf)),
                  pl.BlockSpec((tf, d), lambda i, f: (f, 0)),
                  pl.BlockSpec((tm, d), lambda i, f: (i, 0)),
                  pl.BlockSpec((None, 1, d), lambda i, f: (i // bps, 0, 0)),
                  pl.BlockSpec((1, d), lambda i, f: (0, 0))],
        out_specs=pl.BlockSpec((tm, d), lambda i, f: (i, 0)),
        out_shape=jax.ShapeDtypeStruct((t, d), F32),
        compiler_params=_params("arbitrary", "arbitrary"),
        name="ffn",
    )(h2, w_in, w_in, w_out, x1, gate, gn)


def _s5_param_layout(lam_re, lam_im, log_step, b_re, b_im, c_re, c_im, d_skip):
    groups = lam_re.shape[1]
    both = lambda p: jnp.concatenate([p[0], p[1]], axis=-1)
    row = lambda p: both(p)[:, None, :]
    ls = jnp.broadcast_to(log_step[:, :, None], log_step.shape + (S5_STATE,))
    bt = lambda p: both(jnp.swapaxes(p, -1, -2))
    eye = jnp.eye(S5_GROUP, dtype=F32)
    dcol = (d_skip.reshape(groups, S5_GROUP, 1) * eye[None]).reshape(groups, S5_GROUP * S5_GROUP, 1)
    return (row(lam_re), row(lam_im), row(ls), bt(b_re), bt(b_im), both(c_re), both(c_im), dcol)


def kernel(x, c, w_ada, b_ada, norm_mix, w_in, s5_lambda_re, s5_lambda_im, s5_log_step,
           s5_b_re, s5_b_im, s5_c_re, s5_c_im, s5_d, w_s5_glu, w_branch_s5, w_branch_fnet,
           w_out, norm_ffn, w_ffn_in, w_ffn_out, norm_final):
    batch, seq, d = x.shape
    t = batch * seq
    w5 = s5_d.shape[-1]
    wf = w_branch_fnet.shape[1]
    n1 = FFT_N1
    n2 = seq // n1
    assert w_in.shape[0] == 1, "multi-layer stacks are not supported"
    assert seq % ROW_TILE == 0 and seq % (n1 * BF16_ROWS) == 0
    assert (n2 * wf) % FFT1_TILE == 0 and n2 % BF16_ROWS == 0

    dft = jnp.asarray(_chan_dft_table()).astype(BF16)
    f1 = jnp.asarray(_fft1_table(n1, seq)).astype(BF16)
    m2 = _fft2_table(n1, n2)

    xs = x.reshape(t, d).astype(F32)
    ct = c.astype(F32).T
    for l in range(w_in.shape[0]):
        mod = _ada(ct, w_ada[l], b_ada[l])
        sh_m, sc_m, g_m, sh_f, sc_f, g_f = [m.reshape(batch, 1, d)
                                            for m in jnp.split(mod, N_MOD, axis=-1)]
        wl = w_in[l]
        us5, v, gs, gf = _inproj(
            xs, norm_mix[l].reshape(1, d), sc_m, sh_m,
            wl[:, :w5].T.astype(BF16), wl[:, w5:w5 + wf].astype(BF16),
            wl[:, w5 + wf:].astype(BF16), dft, batch, seq)

        s5p = _s5_param_layout(s5_lambda_re[l], s5_lambda_im[l], s5_log_step[l], s5_b_re[l],
                               s5_b_im[l], s5_c_re[l], s5_c_im[l], s5_d[l])
        ys5 = _s5(us5.reshape(t // CHUNK, w5 * CHUNK), *s5p, chunks_per_seq=seq // CHUNK)

        a = _fft1(f1, v.reshape(batch, 2 * n1, n2 * wf))
        yfn = _fft2(m2, a.reshape(batch, 2, n1, n2, wf))

        xs, h2 = _merge(
            xs, ys5.reshape(t // CHUNK, w5, CHUNK), yfn.reshape(t, wf), gs, gf, g_m,
            norm_ffn[l].reshape(1, d), sc_f, sh_f, w_s5_glu[l].astype(BF16),
            w_branch_s5[l].astype(BF16), w_branch_fnet[l].astype(BF16), w_out[l].astype(BF16), seq)
        xs = _ffn(h2, w_ffn_in[l].astype(BF16), w_ffn_out[l].astype(BF16), xs, g_f,
                  norm_final.reshape(1, d), seq)
    return xs.reshape(batch, seq, d).astype(x.dtype)
```

```python
import functools
import math

import jax
import jax.numpy as jnp
import numpy as np
from jax import lax
from jax.experimental import pallas as pl
from jax.experimental.pallas import tpu as pltpu

F32 = jnp.float32
BF16 = jnp.bfloat16

S5_GROUP = 16
S5_STATE = 64
FNET_GROUP = 256
N_MOD = 6
EPS = 1e-6

LANES = 128
SUBLANES = 8
BF16_ROWS = 16
VMEM_LIMIT_BYTES = 60 * 1024 * 1024

CHUNK = LANES

ROW_TILE = 512
FFN_ROW_TILE = 1024
FFN_TILE = 512
ADA_TILE = 1024

FFT_GROUP = SUBLANES
FFT_N2 = ROW_TILE // FFT_GROUP
FFT1_BATCH = 8


def _params(*semantics):
    return pltpu.CompilerParams(dimension_semantics=semantics, vmem_limit_bytes=VMEM_LIMIT_BYTES)


def _const_spec(shape):
    zeros = (0,) * len(shape)
    return pl.BlockSpec(shape, lambda *_: zeros, pipeline_mode=pl.Buffered(1))


def _dot(a, b):
    return jnp.dot(a, b, preferred_element_type=F32)


def _dot_nt(a, b, precision=None):
    return lax.dot_general(a, b, (((1,), (1,)), ((), ())), precision=precision,
                           preferred_element_type=F32)


def _rms_modulate(x, gain, scale, shift):
    ms = jnp.mean(x * x, axis=-1, keepdims=True)
    y = x * lax.rsqrt(ms + EPS)
    return (y * gain) * (1.0 + scale) + shift


def _ada_kernel(ct_ref, w_ref, b_ref, o_ref, cb_scr):
    d, nb = ct_ref.shape
    tn = o_ref.shape[1]

    @pl.when(pl.program_id(0) == 0)
    def _():
        cs = ct_ref[...]
        cs = cs * jax.nn.sigmoid(cs)
        for b in range(nb):
            cb_scr[b] = jnp.broadcast_to(cs[:, b:b + 1], (d, LANES))

    def body(k, accs):
        k0 = pl.multiple_of(k * SUBLANES, SUBLANES)
        w = w_ref[pl.ds(k0, SUBLANES), :]
        return tuple(acc + w * jnp.tile(cb_scr[b, pl.ds(k0, SUBLANES), :], (1, tn // LANES))
                     for b, acc in enumerate(accs))

    init = tuple(jnp.zeros((SUBLANES, tn), F32) for _ in range(nb))
    accs = lax.fori_loop(0, d // SUBLANES, body, init, unroll=8)
    for b, acc in enumerate(accs):
        o_ref[b:b + 1, :] = jnp.sum(acc, axis=0, keepdims=True) + b_ref[...]


def _ada(ct, w, bias):
    d, nb = ct.shape
    n = w.shape[1]
    return pl.pallas_call(
        _ada_kernel,
        grid=(n // ADA_TILE,),
        in_specs=[pl.BlockSpec((d, nb), lambda j: (0, 0)),
                  pl.BlockSpec((d, ADA_TILE), lambda j: (0, j)),
                  pl.BlockSpec((1, ADA_TILE), lambda j: (0, j))],
        out_specs=pl.BlockSpec((nb, ADA_TILE), lambda j: (0, j)),
        out_shape=jax.ShapeDtypeStruct((nb, n), F32),
        scratch_shapes=[pltpu.VMEM((nb, d, LANES), F32)],
        compiler_params=_params("arbitrary"),
        name="ada",
    )(ct, w, bias.reshape(1, n))


def _inproj_kernel(x_ref, g_ref, sc_ref, sh_ref, ws5t_ref, wfn_ref, wg_ref, perm_ref, dft_ref,
                   us5_ref, v_ref, gs_ref, gf_ref):
    tm, d = x_ref.shape
    hb = _rms_modulate(x_ref[...], g_ref[...], sc_ref[...], sh_ref[...]).astype(BF16)

    ut = _dot_nt(ws5t_ref[...], hb)
    for ci in range(tm // CHUNK):
        us5_ref[ci] = ut[:, ci * CHUNK:(ci + 1) * CHUNK].astype(BF16)

    uf = _dot(hb, wfn_ref[...]).astype(BF16)
    ufp = _dot(perm_ref[...], uf).astype(BF16)
    n2 = tm // FFT_GROUP
    for g in range(uf.shape[1] // FNET_GROUP):
        cols = slice(g * FNET_GROUP, (g + 1) * FNET_GROUP)
        pq = _dot(ufp[:, cols], dft_ref[...])
        re = pq[:, :FNET_GROUP].reshape(n2, FFT_GROUP, FNET_GROUP)
        im = pq[:, FNET_GROUP:].reshape(n2, FFT_GROUP, FNET_GROUP)
        v_ref[:, :, cols] = jnp.concatenate([re, im], axis=1).astype(BF16)

    nb = 512
    for n0 in range(0, 2 * d, nb):
        s = jax.nn.sigmoid(_dot(hb, wg_ref[:, n0:n0 + nb])).astype(BF16)
        if n0 < d:
            gs_ref[:, n0:n0 + nb] = s
        else:
            gf_ref[:, n0 - d:n0 - d + nb] = s


def _inproj(x2d, gain, scale, shift, ws5t, wfn, wg, perm, dft, batch, seq):
    t, d = x2d.shape
    tm = ROW_TILE
    bps = seq // tm
    w5 = ws5t.shape[0]
    wf = wfn.shape[1]
    vrows = 2 * FFT_GROUP
    per_batch = pl.BlockSpec((None, 1, d), lambda i: (i // bps, 0, 0))
    return pl.pallas_call(
        _inproj_kernel,
        grid=(t // tm,),
        in_specs=[pl.BlockSpec((tm, d), lambda i: (i, 0)),
                  _const_spec((1, d)), per_batch, per_batch,
                  _const_spec(ws5t.shape), _const_spec(wfn.shape), _const_spec(wg.shape),
                  _const_spec(perm.shape), _const_spec(dft.shape)],
        out_specs=[pl.BlockSpec((tm // CHUNK, w5, CHUNK), lambda i: (i, 0, 0)),
                   pl.BlockSpec((None, FFT_N2, vrows, wf), lambda i: (i // bps, 0, i % bps, 0)),
                   pl.BlockSpec((tm, d), lambda i: (i, 0)),
                   pl.BlockSpec((tm, d), lambda i: (i, 0))],
        out_shape=[jax.ShapeDtypeStruct((t // CHUNK, w5, CHUNK), BF16),
                   jax.ShapeDtypeStruct((batch, FFT_N2, bps * vrows, wf), BF16),
                   jax.ShapeDtypeStruct((t, d), BF16),
                   jax.ShapeDtypeStruct((t, d), BF16)],
        compiler_params=_params("arbitrary"),
        name="inproj",
    )(x2d, gain, scale, shift, ws5t, wfn, wg, perm, dft)


def _s5_kernel(x_ref, lre_ref, lim_ref, ls_ref, btr_ref, bti_ref, cr_ref, ci_ref, dcol_ref,
               y_ref, v_scr, w_scr, win_scr, wout_scr, *, chunks_per_seq):
    nc = x_ref.shape[0]
    ns = 2 * S5_STATE
    hg = S5_GROUP
    lane = lax.broadcasted_iota(jnp.int32, (1, ns), 1)
    fwd = lane < S5_STATE

    lam_re = lre_ref[...]
    lam_im = lim_ref[...]
    dt = jnp.exp(ls_ref[...])
    lr = lam_re * dt
    li = lam_im * dt

    def powers(e):
        mag = jnp.exp(e * lr)
        return mag * jnp.cos(e * li), mag * jnp.sin(e * li)

    lb_re, lb_im = powers(jnp.ones((1, ns), F32))
    den = lam_re * lam_re + lam_im * lam_im
    num_re = lb_re - 1.0
    num_im = lb_im
    coef_re = (num_re * lam_re + num_im * lam_im) / den
    coef_im = (num_im * lam_re - num_re * lam_im) / den
    btr = btr_ref[...]
    bti = bti_ref[...]
    bbr = coef_re * btr - coef_im * bti
    bbi = coef_re * bti + coef_im * btr
    cr = cr_ref[...]
    ci = ci_ref[...]

    qr = jnp.concatenate([cr[h:h + 1] * bbr - ci[h:h + 1] * bbi for h in range(hg)], axis=0)
    qi = jnp.concatenate([cr[h:h + 1] * bbi + ci[h:h + 1] * bbr for h in range(hg)], axis=0)

    m_idx = lax.broadcasted_iota(jnp.int32, (2 * CHUNK, ns), 0)
    pr, pi = powers(jnp.abs(m_idx - CHUNK).astype(F32))
    active = (m_idx >= jnp.where(fwd, CHUNK, 0)) & (m_idx <= jnp.where(fwd, 2 * CHUNK, CHUNK))
    pr = jnp.where(active, pr, 0.0)
    pi = jnp.where(active, pi, 0.0)
    hi = lax.Precision.HIGHEST
    v = _dot_nt(qr, pr, hi) - _dot_nt(qi, pi, hi)
    centre = lax.broadcasted_iota(jnp.int32, (1, 2 * CHUNK), 1) == CHUNK
    v_scr[...] = v + jnp.where(centre, dcol_ref[...], 0.0)

    def expand(hp, carry):
        for hl in range(2):
            for hq in range(hg):
                r = (2 * hp + hl) * hg + hq
                row = jnp.broadcast_to(v_scr[pl.ds(r, 1), :], (CHUNK, 2 * CHUNK))
                blk = pltpu.roll(row, CHUNK, 1, stride=1, stride_axis=0)
                w_scr[hp, hq * CHUNK:(hq + 1) * CHUNK, hl * CHUNK:(hl + 1) * CHUNK] = (
                    blk[:, :CHUNK].astype(BF16))
        return carry

    lax.fori_loop(0, hg // 2, expand, 0)

    j_idx = lax.broadcasted_iota(jnp.int32, (CHUNK, ns), 0)
    pin_r, pin_i = powers(jnp.where(fwd, CHUNK - 1 - j_idx, j_idx).astype(F32))
    for hq in range(hg):
        rows = slice(hq * CHUNK, (hq + 1) * CHUNK)
        br = bbr[hq:hq + 1]
        bi = bbi[hq:hq + 1]
        win_scr[rows, :ns] = (pin_r * br - pin_i * bi).astype(BF16)
        win_scr[rows, ns:] = (pin_r * bi + pin_i * br).astype(BF16)

    po_r, po_i = powers(jnp.where(fwd, j_idx + 1, CHUNK - j_idx).astype(F32))
    for h in range(hg):
        rows = slice(h * CHUNK, (h + 1) * CHUNK)
        c_r = cr[h:h + 1]
        c_i = ci[h:h + 1]
        wout_scr[rows, :ns] = (c_r * po_r - c_i * po_i).astype(BF16)
        wout_scr[rows, ns:] = (-(c_r * po_i + c_i * po_r)).astype(BF16)

    x = x_ref[...]
    z = _dot(x, win_scr[...])
    zr = z[:, :ns]
    zi = z[:, ns:]

    pos = lax.broadcasted_iota(jnp.int32, (nc, ns), 0) & (chunks_per_seq - 1)

    def in_seq(dist):
        return ((pos >= jnp.where(fwd, dist, 0))
                & (pos < jnp.where(fwd, chunks_per_seq, chunks_per_seq - dist)))

    def neighbour(a, dist):
        return jnp.where(fwd, pltpu.roll(a, dist, 0), pltpu.roll(a, nc - dist, 0))

    dist = 1
    while dist < chunks_per_seq:
        ar, ai = powers(jnp.full((1, ns), float(CHUNK * dist), F32))
        nr = neighbour(zr, dist)
        ni = neighbour(zi, dist)
        ok = in_seq(dist)
        zr, zi = (zr + jnp.where(ok, ar * nr - ai * ni, 0.0),
                  zi + jnp.where(ok, ar * ni + ai * nr, 0.0))
        dist *= 2
    ok = in_seq(1)
    sr = jnp.where(ok, neighbour(zr, 1), 0.0)
    si = jnp.where(ok, neighbour(zi, 1), 0.0)
    s_in = jnp.concatenate([sr, si], axis=1).astype(BF16)
    y_state = _dot_nt(s_in, wout_scr[...])

    wide = 2 * CHUNK
    for hp in range(hg // 2):
        cols = slice(hp * wide, (hp + 1) * wide)
        y_ref[:, cols] = _dot(x, w_scr[hp]) + y_state[:, cols]


def _s5(us5, lre, lim, ls, btr, bti, cr, ci, dcol, chunks_per_seq):
    nc, width = us5.shape
    groups = lre.shape[0]
    gw = S5_GROUP * CHUNK
    ns = 2 * S5_STATE
    assert chunks_per_seq & (chunks_per_seq - 1) == 0
    row = pl.BlockSpec((None, 1, ns), lambda g: (g, 0, 0))
    mat = pl.BlockSpec((None, S5_GROUP, ns), lambda g: (g, 0, 0))
    return pl.pallas_call(
        functools.partial(_s5_kernel, chunks_per_seq=chunks_per_seq),
        grid=(groups,),
        in_specs=[pl.BlockSpec((nc, gw), lambda g: (0, g)),
                  row, row, row, mat, mat, mat, mat,
                  pl.BlockSpec((None, S5_GROUP * S5_GROUP, 1), lambda g: (g, 0, 0))],
        out_specs=pl.BlockSpec((nc, gw), lambda g: (0, g)),
        out_shape=jax.ShapeDtypeStruct((nc, width), F32),
        scratch_shapes=[pltpu.VMEM((S5_GROUP * S5_GROUP, 2 * CHUNK), F32),
                        pltpu.VMEM((S5_GROUP // 2, gw, 2 * CHUNK), BF16),
                        pltpu.VMEM((gw, 2 * ns), BF16),
                        pltpu.VMEM((gw, 2 * ns), BF16)],
        compiler_params=_params("arbitrary"),
        name="s5",
    )(us5, lre, lim, ls, btr, bti, cr, ci, dcol)


def _fft1_kernel(f_ref, v_ref, a_ref):
    rows = a_ref.shape[2]
    for j in range(v_ref.shape[0]):
        res = _dot(f_ref[...], v_ref[j]).astype(BF16)
        for q in range(a_ref.shape[0]):
            a_ref[q, j] = res[q * rows:(q + 1) * rows]


def _fft1(f1, v):
    batch, n2, rows, c = v.shape
    nq = FFT1_BATCH
    tile = 2 * FFT_GROUP
    return pl.pallas_call(
        _fft1_kernel,
        grid=(batch, n2 // nq),
        in_specs=[_const_spec(f1.shape),
                  pl.BlockSpec((None, nq, rows, c), lambda b, j: (b, j, 0, 0))],
        out_specs=pl.BlockSpec((None, rows // tile, nq, tile, c), lambda b, j: (b, 0, j, 0, 0)),
        out_shape=jax.ShapeDtypeStruct((batch, rows // tile, n2, tile, c), BF16),
        compiler_params=_params("arbitrary", "arbitrary"),
        name="fft1",
    )(f1, v)


def _fft2_kernel(m_ref, a_ref, o_ref):
    n2, kb, c = o_ref.shape
    parts = []
    for qq in range(a_ref.shape[0]):
        rhs = a_ref[qq].reshape(n2 * kb, c)
        parts.append(_dot(m_ref[qq], rhs).reshape(n2, FFT_GROUP, c))
    o_ref[...] = jnp.concatenate(parts, axis=1).astype(BF16)


def _fft2(m2, a):
    batch, nq, n2, tile, c = a.shape
    per = BF16_ROWS // FFT_GROUP
    return pl.pallas_call(
        _fft2_kernel,
        grid=(nq // per, batch),
        in_specs=[pl.BlockSpec((per,) + m2.shape[1:], lambda k, b: (k, 0, 0)),
                  pl.BlockSpec((None, per, n2, tile, c), lambda k, b: (b, k, 0, 0, 0))],
        out_specs=pl.BlockSpec((None, n2, BF16_ROWS, c), lambda k, b: (b, 0, k, 0)),
        out_shape=jax.ShapeDtypeStruct((batch, n2, nq * FFT_GROUP, c), BF16),
        compiler_params=_params("arbitrary", "arbitrary"),
        name="fft2",
    )(m2, a)


@functools.lru_cache(maxsize=None)
def _chan_dft_table():
    n = FNET_GROUP
    idx = np.arange(n)
    ang = 2.0 * np.pi * ((idx[:, None] * idx[None, :]) % n) / n
    return (np.concatenate([np.cos(ang), -np.sin(ang)], axis=1) / math.sqrt(n)).astype(np.float32)


@functools.lru_cache(maxsize=None)
def _perm_table(tm):
    n2 = tm // FFT_GROUP
    dst = np.arange(tm)
    p = np.zeros((tm, tm), np.float32)
    p[dst, (dst % FFT_GROUP) * n2 + dst // FFT_GROUP] = 1.0
    return p.astype(BF16)


@functools.lru_cache(maxsize=None)
def _fft1_table(n1, seq):
    g = FFT_GROUP
    idx = np.arange(n1)
    ang = 2.0 * np.pi * ((idx[:, None] * idx[None, :]) % n1) / n1
    c, s = np.cos(ang), np.sin(ang)
    blocks = np.array([[c, s], [-s, c]])
    t = blocks.reshape(2, 2, n1 // g, g, n1 // g, g).transpose(2, 0, 3, 4, 1, 5)
    return (t.reshape(2 * n1, 2 * n1) / math.sqrt(seq)).astype(np.float32)


@functools.lru_cache(maxsize=None)
def _fft2_table(n1, n2):
    g = FFT_GROUP
    n = n1 * n2
    q = np.arange(n1 // g)[:, None, None, None]
    k2 = np.arange(n2)[None, :, None, None]
    k1l = np.arange(g)[None, None, :, None]
    nn = np.arange(n2)[None, None, None, :]
    ang = 2.0 * np.pi * ((nn * (g * q + k1l + n1 * k2)) % n) / n
    cs = np.stack([np.cos(ang), np.sin(ang)], axis=-1).astype(np.float32)
    m = np.zeros((n1 // g, n2, g, n2, 2, g), np.float32)
    for l in range(g):
        m[:, :, l, :, :, l] = cs[:, :, l]
    return m.reshape(n1 // g, n2 * g, n2 * 2 * g)


def _merge_kernel(x_ref, ys5_ref, yfn_ref, gs_ref, gf_ref, gm_ref, g2_ref, sc_ref, sh_ref,
                  wglu_ref, wbs_ref, wbf_ref, wo_ref, x1_ref, h2_ref, m_scr):
    tm, d = x_ref.shape
    w5 = wbs_ref.shape[0]
    parts = [jax.nn.gelu(ys5_ref[ci].T, approximate=True).astype(BF16)
             for ci in range(tm // CHUNK)]
    ge = jnp.concatenate(parts, axis=0)
    z = _dot(ge, wglu_ref[...])
    ys = (z[:, :w5] * jax.nn.sigmoid(z[:, w5:])).astype(BF16)
    yfn = yfn_ref[...]
    nb = 512
    for n0 in range(0, d, nb):
        cols = slice(n0, n0 + nb)
        m_scr[:, cols] = (gs_ref[:, cols].astype(F32) * _dot(ys, wbs_ref[:, cols])
                          + gf_ref[:, cols].astype(F32) * _dot(yfn, wbf_ref[:, cols])).astype(BF16)
    merged = m_scr[...]
    for n0 in range(0, d, nb):
        cols = slice(n0, n0 + nb)
        x1_ref[:, cols] = x_ref[:, cols] + gm_ref[:, cols] * _dot(merged, wo_ref[:, cols])
    h2_ref[...] = _rms_modulate(x1_ref[...], g2_ref[...], sc_ref[...], sh_ref[...]).astype(BF16)


def _merge(x2d, ys5, yfn, gs, gf, gm, g2, scale, shift, wglu, wbs, wbf, wo, seq):
    t, d = x2d.shape
    tm = ROW_TILE
    bps = seq // tm
    w5 = ys5.shape[1]
    wf = yfn.shape[1]
    rows = lambda width: pl.BlockSpec((tm, width), lambda i: (i, 0))
    per_batch = pl.BlockSpec((None, 1, d), lambda i: (i // bps, 0, 0))
    return pl.pallas_call(
        _merge_kernel,
        grid=(t // tm,),
        in_specs=[rows(d),
                  pl.BlockSpec((tm // CHUNK, w5, CHUNK), lambda i: (i, 0, 0)),
                  rows(wf), rows(d), rows(d),
                  per_batch, _const_spec((1, d)), per_batch, per_batch,
                  _const_spec(wglu.shape), _const_spec(wbs.shape), _const_spec(wbf.shape),
                  _const_spec(wo.shape)],
        out_specs=[rows(d), rows(d)],
        out_shape=[jax.ShapeDtypeStruct((t, d), F32), jax.ShapeDtypeStruct((t, d), BF16)],
        scratch_shapes=[pltpu.VMEM((tm, d), BF16)],
        compiler_params=_params("arbitrary"),
        name="merge",
    )(x2d, ys5, yfn, gs, gf, gm, g2, scale, shift, wglu, wbs, wbf, wo)


def _ffn_kernel(h_ref, wa_ref, wb_ref, wo_ref, x1_hbm, gate_ref, gn_ref, o_ref, x1_buf, x1_sem):
    i = pl.program_id(0)
    f = pl.program_id(1)
    tm, d = o_ref.shape

    def x1_copy():
        return pltpu.make_async_copy(x1_hbm.at[pl.ds(i * tm, tm), :], x1_buf, x1_sem)

    @pl.when(f == 0)
    def _():
        x1_copy().start()
        o_ref[...] = jnp.zeros_like(o_ref)

    h = h_ref[...]
    a = _dot(h, wa_ref[...])
    b = _dot(h, wb_ref[...])
    act = (a * jax.nn.sigmoid(a) * b).astype(BF16)
    nb = 512
    for n0 in range(0, d, nb):
        o_ref[:, n0:n0 + nb] += _dot(act, wo_ref[:, n0:n0 + nb])

    @pl.when(f == pl.num_programs(1) - 1)
    def _():
        x1_copy().wait()
        x2 = x1_buf[...] + gate_ref[...] * o_ref[...]
        ms = jnp.mean(x2 * x2, axis=-1, keepdims=True)
        o_ref[...] = (x2 * lax.rsqrt(ms + EPS)) * gn_ref[...]


def _ffn(h2, w_in, w_out, x1, gate, gn, seq):
    t, d = h2.shape
    ff = w_out.shape[0]
    tm, tf = FFN_ROW_TILE, FFN_TILE
    bps = seq // tm
    nf = ff // tf
    return pl.pallas_call(
        _ffn_kernel,
        grid=(t // tm, nf),
        in_specs=[pl.BlockSpec((tm, d), lambda i, f: (i, 0)),
                  pl.BlockSpec((d, tf), lambda i, f: (0, f)),
                  pl.BlockSpec((d, tf), lambda i, f: (0, f + nf)),
                  pl.BlockSpec((tf, d), lambda i, f: (f, 0)),
                  pl.BlockSpec(memory_space=pl.ANY),
                  pl.BlockSpec((None, 1, d), lambda i, f: (i // bps, 0, 0)),
                  pl.BlockSpec((1, d), lambda i, f: (0, 0))],
        out_specs=pl.BlockSpec((tm, d), lambda i, f: (i, 0)),
        out_shape=jax.ShapeDtypeStruct((t, d), F32),
        scratch_shapes=[pltpu.VMEM((tm, d), F32), pltpu.SemaphoreType.DMA],
        compiler_params=_params("arbitrary", "arbitrary"),
        name="ffn",
    )(h2, w_in, w_in, w_out, x1, gate, gn)


def _s5_param_layout(lam_re, lam_im, log_step, b_re, b_im, c_re, c_im, d_skip):
    groups = lam_re.shape[1]
    both = lambda p: jnp.concatenate([p[0], p[1]], axis=-1)
    row = lambda p: both(p)[:, None, :]
    ls = jnp.broadcast_to(log_step[:, :, None], log_step.shape + (S5_STATE,))
    bt = lambda p: both(jnp.swapaxes(p, -1, -2))
    eye = jnp.eye(S5_GROUP, dtype=F32)
    dcol = (d_skip.reshape(groups, S5_GROUP, 1) * eye[None]).reshape(groups, S5_GROUP * S5_GROUP, 1)
    return (row(lam_re), row(lam_im), row(ls), bt(b_re), bt(b_im), both(c_re), both(c_im), dcol)


def kernel(x, c, w_ada, b_ada, norm_mix, w_in, s5_lambda_re, s5_lambda_im, s5_log_step,
           s5_b_re, s5_b_im, s5_c_re, s5_c_im, s5_d, w_s5_glu, w_branch_s5, w_branch_fnet,
           w_out, norm_ffn, w_ffn_in, w_ffn_out, norm_final):
    batch, seq, d = x.shape
    t = batch * seq
    w5 = s5_d.shape[-1]
    wf = w_branch_fnet.shape[1]
    n2 = FFT_N2
    n1 = seq // n2
    assert w_in.shape[0] == 1, "multi-layer stacks are not supported"
    assert seq % FFN_ROW_TILE == 0 and n1 % BF16_ROWS == 0 and n2 % FFT1_BATCH == 0

    dft = jnp.asarray(_chan_dft_table()).astype(BF16)
    perm = jnp.asarray(_perm_table(ROW_TILE))
    f1 = jnp.asarray(_fft1_table(n1, seq)).astype(BF16)
    m2 = jnp.asarray(_fft2_table(n1, n2)).astype(BF16)

    xs = x.reshape(t, d).astype(F32)
    ct = c.astype(F32).T
    for l in range(w_in.shape[0]):
        mod = _ada(ct, w_ada[l], b_ada[l])
        sh_m, sc_m, g_m, sh_f, sc_f, g_f = [m.reshape(batch, 1, d)
                                            for m in jnp.split(mod, N_MOD, axis=-1)]
        wl = w_in[l]
        us5, v, gs, gf = _inproj(
            xs, norm_mix[l].reshape(1, d), sc_m, sh_m,
            wl[:, :w5].T.astype(BF16), wl[:, w5:w5 + wf].astype(BF16),
            wl[:, w5 + wf:].astype(BF16), perm, dft, batch, seq)

        s5p = _s5_param_layout(s5_lambda_re[l], s5_lambda_im[l], s5_log_step[l], s5_b_re[l],
                               s5_b_im[l], s5_c_re[l], s5_c_im[l], s5_d[l])
        ys5 = _s5(us5.reshape(t // CHUNK, w5 * CHUNK), *s5p, chunks_per_seq=seq // CHUNK)

        yfn = _fft2(m2, _fft1(f1, v))

        xs, h2 = _merge(
            xs, ys5.reshape(t // CHUNK, w5, CHUNK), yfn.reshape(t, wf), gs, gf, g_m,
            norm_ffn[l].reshape(1, d), sc_f, sh_f, w_s5_glu[l].astype(BF16),
            w_branch_s5[l].astype(BF16), w_branch_fnet[l].astype(BF16), w_out[l].astype(BF16), seq)
        xs = _ffn(h2, w_ffn_in[l].astype(BF16), w_ffn_out[l].astype(BF16), xs, g_f,
                  norm_final.reshape(1, d), seq)
    return xs.reshape(batch, seq, d).astype(x.dtype)
```

```python
import functools
import math

import jax
import jax.numpy as jnp
import numpy as np
from jax import lax
from jax.experimental import pallas as pl
from jax.experimental.pallas import tpu as pltpu

F32 = jnp.float32
BF16 = jnp.bfloat16

S5_GROUP = 16
S5_STATE = 64
FNET_GROUP = 256
N_MOD = 6
EPS = 1e-6

LANES = 128
SUBLANES = 8
BF16_ROWS = 16
VMEM_LIMIT_BYTES = 60 * 1024 * 1024

CHUNK = LANES

ROW_TILE = 512
FFN_ROW_TILE = 1024
FFN_TILE = 512
ADA_TILE = 1024

FFT_GROUP = SUBLANES
FFT_N2 = ROW_TILE // FFT_GROUP


def _params(*semantics):
    return pltpu.CompilerParams(dimension_semantics=semantics, vmem_limit_bytes=VMEM_LIMIT_BYTES)


def _const_spec(shape):
    zeros = (0,) * len(shape)
    return pl.BlockSpec(shape, lambda *_: zeros, pipeline_mode=pl.Buffered(1))


def _dot(a, b):
    return jnp.dot(a, b, preferred_element_type=F32)


def _dot_nt(a, b, precision=None):
    return lax.dot_general(a, b, (((1,), (1,)), ((), ())), precision=precision,
                           preferred_element_type=F32)


def _rms_modulate(x, gain, scale, shift):
    ms = jnp.mean(x * x, axis=-1, keepdims=True)
    y = x * lax.rsqrt(ms + EPS)
    return (y * gain) * (1.0 + scale) + shift


def _ada_kernel(ct_ref, w_ref, b_ref, o_ref, cb_scr):
    d, nb = ct_ref.shape
    tn = o_ref.shape[1]

    @pl.when(pl.program_id(0) == 0)
    def _():
        cs = ct_ref[...]
        cs = cs * jax.nn.sigmoid(cs)
        for b in range(nb):
            cb_scr[b] = jnp.broadcast_to(cs[:, b:b + 1], (d, LANES))

    def body(k, accs):
        k0 = pl.multiple_of(k * SUBLANES, SUBLANES)
        w = w_ref[pl.ds(k0, SUBLANES), :]
        return tuple(acc + w * jnp.tile(cb_scr[b, pl.ds(k0, SUBLANES), :], (1, tn // LANES))
                     for b, acc in enumerate(accs))

    init = tuple(jnp.zeros((SUBLANES, tn), F32) for _ in range(nb))
    accs = lax.fori_loop(0, d // SUBLANES, body, init, unroll=8)
    for b, acc in enumerate(accs):
        o_ref[b:b + 1, :] = jnp.sum(acc, axis=0, keepdims=True) + b_ref[...]


def _ada(ct, w, bias):
    d, nb = ct.shape
    n = w.shape[1]
    return pl.pallas_call(
        _ada_kernel,
        grid=(n // ADA_TILE,),
        in_specs=[pl.BlockSpec((d, nb), lambda j: (0, 0)),
                  pl.BlockSpec((d, ADA_TILE), lambda j: (0, j)),
                  pl.BlockSpec((1, ADA_TILE), lambda j: (0, j))],
        out_specs=pl.BlockSpec((nb, ADA_TILE), lambda j: (0, j)),
        out_shape=jax.ShapeDtypeStruct((nb, n), F32),
        scratch_shapes=[pltpu.VMEM((nb, d, LANES), F32)],
        compiler_params=_params("arbitrary"),
        name="ada",
    )(ct, w, bias.reshape(1, n))


def _inproj_kernel(x_ref, g_ref, sc_ref, sh_ref, ws5t_ref, wfn_ref, wg_ref, perm_ref, dft_ref,
                   us5_ref, v_ref, gs_ref, gf_ref):
    tm, d = x_ref.shape
    hb = _rms_modulate(x_ref[...], g_ref[...], sc_ref[...], sh_ref[...]).astype(BF16)

    ut = _dot_nt(ws5t_ref[...], hb)
    for ci in range(tm // CHUNK):
        us5_ref[ci] = ut[:, ci * CHUNK:(ci + 1) * CHUNK].astype(BF16)

    uf = _dot(hb, wfn_ref[...]).astype(BF16)
    ufp = _dot(perm_ref[...], uf).astype(BF16)
    n2 = tm // FFT_GROUP
    for g in range(uf.shape[1] // FNET_GROUP):
        cols = slice(g * FNET_GROUP, (g + 1) * FNET_GROUP)
        pq = _dot(ufp[:, cols], dft_ref[...])
        re = pq[:, :FNET_GROUP].reshape(n2, FFT_GROUP, FNET_GROUP)
        im = pq[:, FNET_GROUP:].reshape(n2, FFT_GROUP, FNET_GROUP)
        v_ref[:, :, cols] = jnp.concatenate([re, im], axis=1).astype(BF16)

    nb = 512
    for n0 in range(0, 2 * d, nb):
        s = jax.nn.sigmoid(_dot(hb, wg_ref[:, n0:n0 + nb])).astype(BF16)
        if n0 < d:
            gs_ref[:, n0:n0 + nb] = s
        else:
            gf_ref[:, n0 - d:n0 - d + nb] = s


def _inproj(x2d, gain, scale, shift, ws5t, wfn, wg, perm, dft, batch, seq):
    t, d = x2d.shape
    tm = ROW_TILE
    bps = seq // tm
    w5 = ws5t.shape[0]
    wf = wfn.shape[1]
    vrows = 2 * FFT_GROUP
    per_batch = pl.BlockSpec((None, 1, d), lambda i: (i // bps, 0, 0))
    return pl.pallas_call(
        _inproj_kernel,
        grid=(t // tm,),
        in_specs=[pl.BlockSpec((tm, d), lambda i: (i, 0)),
                  _const_spec((1, d)), per_batch, per_batch,
                  _const_spec(ws5t.shape), _const_spec(wfn.shape), _const_spec(wg.shape),
                  _const_spec(perm.shape), _const_spec(dft.shape)],
        out_specs=[pl.BlockSpec((tm // CHUNK, w5, CHUNK), lambda i: (i, 0, 0)),
                   pl.BlockSpec((None, FFT_N2, vrows, wf), lambda i: (i // bps, 0, i % bps, 0)),
                   pl.BlockSpec((tm, d), lambda i: (i, 0)),
                   pl.BlockSpec((tm, d), lambda i: (i, 0))],
        out_shape=[jax.ShapeDtypeStruct((t // CHUNK, w5, CHUNK), BF16),
                   jax.ShapeDtypeStruct((batch, FFT_N2, bps * vrows, wf), BF16),
                   jax.ShapeDtypeStruct((t, d), BF16),
                   jax.ShapeDtypeStruct((t, d), BF16)],
        compiler_params=_params("arbitrary"),
        name="inproj",
    )(x2d, gain, scale, shift, ws5t, wfn, wg, perm, dft)


def _s5_kernel(x_ref, lre_ref, lim_ref, ls_ref, btr_ref, bti_ref, cr_ref, ci_ref, dcol_ref,
               y_ref, v_scr, w_scr, win_scr, wout_scr, *, chunks_per_seq):
    nc = x_ref.shape[0]
    ns = 2 * S5_STATE
    hg = S5_GROUP
    lane = lax.broadcasted_iota(jnp.int32, (1, ns), 1)
    fwd = lane < S5_STATE

    lam_re = lre_ref[...]
    lam_im = lim_ref[...]
    dt = jnp.exp(ls_ref[...])
    lr = lam_re * dt
    li = lam_im * dt

    def powers(e):
        mag = jnp.exp(e * lr)
        return mag * jnp.cos(e * li), mag * jnp.sin(e * li)

    lb_re, lb_im = powers(jnp.ones((1, ns), F32))
    den = lam_re * lam_re + lam_im * lam_im
    num_re = lb_re - 1.0
    num_im = lb_im
    coef_re = (num_re * lam_re + num_im * lam_im) / den
    coef_im = (num_im * lam_re - num_re * lam_im) / den
    btr = btr_ref[...]
    bti = bti_ref[...]
    bbr = coef_re * btr - coef_im * bti
    bbi = coef_re * bti + coef_im * btr
    cr = cr_ref[...]
    ci = ci_ref[...]

    qr = jnp.concatenate([cr[h:h + 1] * bbr - ci[h:h + 1] * bbi for h in range(hg)], axis=0)
    qi = jnp.concatenate([cr[h:h + 1] * bbi + ci[h:h + 1] * bbr for h in range(hg)], axis=0)

    m_idx = lax.broadcasted_iota(jnp.int32, (2 * CHUNK, ns), 0)
    pr, pi = powers(jnp.abs(m_idx - CHUNK).astype(F32))
    active = (m_idx >= jnp.where(fwd, CHUNK, 0)) & (m_idx <= jnp.where(fwd, 2 * CHUNK, CHUNK))
    pr = jnp.where(active, pr, 0.0)
    pi = jnp.where(active, pi, 0.0)
    hi = lax.Precision.HIGHEST
    v = _dot_nt(qr, pr, hi) - _dot_nt(qi, pi, hi)
    centre = lax.broadcasted_iota(jnp.int32, (1, 2 * CHUNK), 1) == CHUNK
    v_scr[...] = v + jnp.where(centre, dcol_ref[...], 0.0)

    def expand(hp, carry):
        for hl in range(2):
            for hq in range(hg):
                r = (2 * hp + hl) * hg + hq
                row = jnp.broadcast_to(v_scr[pl.ds(r, 1), :], (CHUNK, 2 * CHUNK))
                blk = pltpu.roll(row, CHUNK, 1, stride=1, stride_axis=0)
                w_scr[hp, hq * CHUNK:(hq + 1) * CHUNK, hl * CHUNK:(hl + 1) * CHUNK] = (
                    blk[:, :CHUNK].astype(BF16))
        return carry

    lax.fori_loop(0, hg // 2, expand, 0)

    j_idx = lax.broadcasted_iota(jnp.int32, (CHUNK, ns), 0)
    pin_r, pin_i = powers(jnp.where(fwd, CHUNK - 1 - j_idx, j_idx).astype(F32))
    for hq in range(hg):
        rows = slice(hq * CHUNK, (hq + 1) * CHUNK)
        br = bbr[hq:hq + 1]
        bi = bbi[hq:hq + 1]
        win_scr[rows, :ns] = (pin_r * br - pin_i * bi).astype(BF16)
        win_scr[rows, ns:] = (pin_r * bi + pin_i * br).astype(BF16)

    po_r, po_i = powers(jnp.where(fwd, j_idx + 1, CHUNK - j_idx).astype(F32))
    for h in range(hg):
        rows = slice(h * CHUNK, (h + 1) * CHUNK)
        c_r = cr[h:h + 1]
        c_i = ci[h:h + 1]
        wout_scr[rows, :ns] = (c_r * po_r - c_i * po_i).astype(BF16)
        wout_scr[rows, ns:] = (-(c_r * po_i + c_i * po_r)).astype(BF16)

    x = x_ref[...]
    z = _dot(x, win_scr[...])
    zr = z[:, :ns]
    zi = z[:, ns:]

    pos = lax.broadcasted_iota(jnp.int32, (nc, ns), 0) & (chunks_per_seq - 1)

    def in_seq(dist):
        return ((pos >= jnp.where(fwd, dist, 0))
                & (pos < jnp.where(fwd, chunks_per_seq, chunks_per_seq - dist)))

    def neighbour(a, dist):
        return jnp.where(fwd, pltpu.roll(a, dist, 0), pltpu.roll(a, nc - dist, 0))

    dist = 1
    while dist < chunks_per_seq:
        ar, ai = powers(jnp.full((1, ns), float(CHUNK * dist), F32))
        nr = neighbour(zr, dist)
        ni = neighbour(zi, dist)
        ok = in_seq(dist)
        zr, zi = (zr + jnp.where(ok, ar * nr - ai * ni, 0.0),
                  zi + jnp.where(ok, ar * ni + ai * nr, 0.0))
        dist *= 2
    ok = in_seq(1)
    sr = jnp.where(ok, neighbour(zr, 1), 0.0)
    si = jnp.where(ok, neighbour(zi, 1), 0.0)
    s_in = jnp.concatenate([sr, si], axis=1).astype(BF16)
    y_state = _dot_nt(s_in, wout_scr[...])

    wide = 2 * CHUNK
    for hp in range(hg // 2):
        cols = slice(hp * wide, (hp + 1) * wide)
        y_ref[:, cols] = _dot(x, w_scr[hp]) + y_state[:, cols]


def _s5(us5, lre, lim, ls, btr, bti, cr, ci, dcol, chunks_per_seq):
    nc, width = us5.shape
    groups = lre.shape[0]
    gw = S5_GROUP * CHUNK
    ns = 2 * S5_STATE
    assert chunks_per_seq & (chunks_per_seq - 1) == 0
    row = pl.BlockSpec((None, 1, ns), lambda g: (g, 0, 0))
    mat = pl.BlockSpec((None, S5_GROUP, ns), lambda g: (g, 0, 0))
    return pl.pallas_call(
        functools.partial(_s5_kernel, chunks_per_seq=chunks_per_seq),
        grid=(groups,),
        in_specs=[pl.BlockSpec((nc, gw), lambda g: (0, g)),
                  row, row, row, mat, mat, mat, mat,
                  pl.BlockSpec((None, S5_GROUP * S5_GROUP, 1), lambda g: (g, 0, 0))],
        out_specs=pl.BlockSpec((nc, gw), lambda g: (0, g)),
        out_shape=jax.ShapeDtypeStruct((nc, width), F32),
        scratch_shapes=[pltpu.VMEM((S5_GROUP * S5_GROUP, 2 * CHUNK), F32),
                        pltpu.VMEM((S5_GROUP // 2, gw, 2 * CHUNK), BF16),
                        pltpu.VMEM((gw, 2 * ns), BF16),
                        pltpu.VMEM((gw, 2 * ns), BF16)],
        compiler_params=_params("arbitrary"),
        name="s5",
    )(us5, lre, lim, ls, btr, bti, cr, ci, dcol)


def _fft_kernel(f_ref, m_ref, v_ref, o_ref, a_scr):
    nq, n2, tile, c = a_scr.shape

    def stage1(j, carry):
        res = _dot(f_ref[...], v_ref[j]).astype(BF16)
        for q in range(nq):
            a_scr[q, j] = res[q * tile:(q + 1) * tile]
        return carry

    lax.fori_loop(0, n2, stage1, 0, unroll=8)

    per = tile // FFT_GROUP
    for k in range(nq // per):
        parts = []
        for qq in range(per):
            q = k * per + qq
            rhs = a_scr[q].reshape(n2 * tile, c)
            parts.append(_dot(m_ref[q], rhs).reshape(n2, FFT_GROUP, c))
        o_ref[:, k * tile:(k + 1) * tile, :] = jnp.concatenate(parts, axis=1).astype(BF16)


def _fft(f1, m2, v):
    batch, n2, rows, c = v.shape
    tile = 2 * FFT_GROUP
    cs = FNET_GROUP
    return pl.pallas_call(
        _fft_kernel,
        grid=(batch, c // cs),
        in_specs=[_const_spec(f1.shape), _const_spec(m2.shape),
                  pl.BlockSpec((None, n2, rows, cs), lambda b, j: (b, 0, 0, j))],
        out_specs=pl.BlockSpec((None, n2, rows // 2, cs), lambda b, j: (b, 0, 0, j)),
        out_shape=jax.ShapeDtypeStruct((batch, n2, rows // 2, c), BF16),
        scratch_shapes=[pltpu.VMEM((rows // tile, n2, tile, cs), BF16)],
        compiler_params=_params("arbitrary", "arbitrary"),
        name="fft",
    )(f1, m2, v)


@functools.lru_cache(maxsize=None)
def _chan_dft_table():
    n = FNET_GROUP
    idx = np.arange(n)
    ang = 2.0 * np.pi * ((idx[:, None] * idx[None, :]) % n) / n
    return (np.concatenate([np.cos(ang), -np.sin(ang)], axis=1) / math.sqrt(n)).astype(np.float32)


@functools.lru_cache(maxsize=None)
def _perm_table(tm):
    n2 = tm // FFT_GROUP
    dst = np.arange(tm)
    p = np.zeros((tm, tm), np.float32)
    p[dst, (dst % FFT_GROUP) * n2 + dst // FFT_GROUP] = 1.0
    return p.astype(BF16)


@functools.lru_cache(maxsize=None)
def _fft1_table(n1, seq):
    g = FFT_GROUP
    idx = np.arange(n1)
    ang = 2.0 * np.pi * ((idx[:, None] * idx[None, :]) % n1) / n1
    c, s = np.cos(ang), np.sin(ang)
    blocks = np.array([[c, s], [-s, c]])
    t = blocks.reshape(2, 2, n1 // g, g, n1 // g, g).transpose(2, 0, 3, 4, 1, 5)
    return (t.reshape(2 * n1, 2 * n1) / math.sqrt(seq)).astype(np.float32)


@functools.lru_cache(maxsize=None)
def _fft2_table(n1, n2):
    g = FFT_GROUP
    n = n1 * n2
    q = np.arange(n1 // g)[:, None, None, None]
    k2 = np.arange(n2)[None, :, None, None]
    k1l = np.arange(g)[None, None, :, None]
    nn = np.arange(n2)[None, None, None, :]
    ang = 2.0 * np.pi * ((nn * (g * q + k1l + n1 * k2)) % n) / n
    cs = np.stack([np.cos(ang), np.sin(ang)], axis=-1).astype(np.float32)
    m = np.zeros((n1 // g, n2, g, n2, 2, g), np.float32)
    for l in range(g):
        m[:, :, l, :, :, l] = cs[:, :, l]
    return m.reshape(n1 // g, n2 * g, n2 * 2 * g)


def _merge_kernel(x_ref, ys5_ref, yfn_ref, gs_ref, gf_ref, gm_ref, g2_ref, sc_ref, sh_ref,
                  wglu_ref, wbs_ref, wbf_ref, wo_ref, x1_ref, h2_ref, m_scr):
    tm, d = x_ref.shape
    w5 = wbs_ref.shape[0]
    parts = [jax.nn.gelu(ys5_ref[ci].T, approximate=True).astype(BF16)
             for ci in range(tm // CHUNK)]
    ge = jnp.concatenate(parts, axis=0)
    z = _dot(ge, wglu_ref[...])
    ys = (z[:, :w5] * jax.nn.sigmoid(z[:, w5:])).astype(BF16)
    yfn = yfn_ref[...]
    nb = 512
    for n0 in range(0, d, nb):
        cols = slice(n0, n0 + nb)
        m_scr[:, cols] = (gs_ref[:, cols].astype(F32) * _dot(ys, wbs_ref[:, cols])
                          + gf_ref[:, cols].astype(F32) * _dot(yfn, wbf_ref[:, cols])).astype(BF16)
    rg = 256
    for r0 in range(0, tm, rg):
        rows = slice(r0, r0 + rg)
        x1 = x_ref[rows, :] + gm_ref[...] * _dot(m_scr[rows, :], wo_ref[...])
        x1_ref[rows, :] = x1
        h2_ref[rows, :] = _rms_modulate(x1, g2_ref[...], sc_ref[...], sh_ref[...]).astype(BF16)


def _merge(x2d, ys5, yfn, gs, gf, gm, g2, scale, shift, wglu, wbs, wbf, wo, seq):
    t, d = x2d.shape
    tm = ROW_TILE
    bps = seq // tm
    w5 = ys5.shape[1]
    wf = yfn.shape[1]
    rows = lambda width: pl.BlockSpec((tm, width), lambda i: (i, 0))
    per_batch = pl.BlockSpec((None, 1, d), lambda i: (i // bps, 0, 0))
    return pl.pallas_call(
        _merge_kernel,
        grid=(t // tm,),
        in_specs=[rows(d),
                  pl.BlockSpec((tm // CHUNK, w5, CHUNK), lambda i: (i, 0, 0)),
                  rows(wf), rows(d), rows(d),
                  per_batch, _const_spec((1, d)), per_batch, per_batch,
                  _const_spec(wglu.shape), _const_spec(wbs.shape), _const_spec(wbf.shape),
                  _const_spec(wo.shape)],
        out_specs=[rows(d), rows(d)],
        out_shape=[jax.ShapeDtypeStruct((t, d), F32), jax.ShapeDtypeStruct((t, d), BF16)],
        scratch_shapes=[pltpu.VMEM((tm, d), BF16)],
        compiler_params=_params("arbitrary"),
        name="merge",
    )(x2d, ys5, yfn, gs, gf, gm, g2, scale, shift, wglu, wbs, wbf, wo)


def _ffn_kernel(h_ref, wa_ref, wb_ref, wo_ref, x1_hbm, gate_ref, gn_ref, o_ref, x1_buf, x1_sem):
    i = pl.program_id(0)
    f = pl.program_id(1)
    tm, d = o_ref.shape

    def x1_copy():
        return pltpu.make_async_copy(x1_hbm.at[pl.ds(i * tm, tm), :], x1_buf, x1_sem)

    def hidden(rows):
        h = h_ref[rows, :]
        a = _dot(h, wa_ref[...])
        b = _dot(h, wb_ref[...])
        return (a * jax.nn.sigmoid(a) * b).astype(BF16)

    nb = 512
    last = pl.num_programs(1) - 1

    @pl.when(f == 0)
    def _():
        x1_copy().start()
        act = hidden(slice(None))
        for n0 in range(0, d, nb):
            o_ref[:, n0:n0 + nb] = _dot(act, wo_ref[:, n0:n0 + nb])

    @pl.when((f > 0) & (f < last))
    def _():
        act = hidden(slice(None))
        for n0 in range(0, d, nb):
            o_ref[:, n0:n0 + nb] += _dot(act, wo_ref[:, n0:n0 + nb])

    @pl.when(f == last)
    def _():
        x1_copy().wait()
        rg = 256
        for r0 in range(0, tm, rg):
            rows = slice(r0, r0 + rg)
            acc = o_ref[rows, :] + _dot(hidden(rows), wo_ref[...])
            x2 = x1_buf[rows, :] + gate_ref[...] * acc
            ms = jnp.mean(x2 * x2, axis=-1, keepdims=True)
            o_ref[rows, :] = (x2 * lax.rsqrt(ms + EPS)) * gn_ref[...]


def _ffn(h2, w_in, w_out, x1, gate, gn, seq):
    t, d = h2.shape
    ff = w_out.shape[0]
    tm, tf = FFN_ROW_TILE, FFN_TILE
    bps = seq // tm
    nf = ff // tf
    assert nf >= 2
    return pl.pallas_call(
        _ffn_kernel,
        grid=(t // tm, nf),
        in_specs=[pl.BlockSpec((tm, d), lambda i, f: (i, 0)),
                  pl.BlockSpec((d, tf), lambda i, f: (0, f)),
                  pl.BlockSpec((d, tf), lambda i, f: (0, f + nf)),
                  pl.BlockSpec((tf, d), lambda i, f: (f, 0)),
                  pl.BlockSpec(memory_space=pl.ANY),
                  pl.BlockSpec((None, 1, d), lambda i, f: (i // bps, 0, 0)),
                  pl.BlockSpec((1, d), lambda i, f: (0, 0))],
        out_specs=pl.BlockSpec((tm, d), lambda i, f: (i, 0)),
        out_shape=jax.ShapeDtypeStruct((t, d), F32),
        scratch_shapes=[pltpu.VMEM((tm, d), F32), pltpu.SemaphoreType.DMA],
        compiler_params=_params("arbitrary", "arbitrary"),
        name="ffn",
    )(h2, w_in, w_in, w_out, x1, gate, gn)


def _s5_param_layout(lam_re, lam_im, log_step, b_re, b_im, c_re, c_im, d_skip):
    groups = lam_re.shape[1]
    both = lambda p: jnp.concatenate([p[0], p[1]], axis=-1)
    row = lambda p: both(p)[:, None, :]
    ls = jnp.broadcast_to(log_step[:, :, None], log_step.shape + (S5_STATE,))
    bt = lambda p: both(jnp.swapaxes(p, -1, -2))
    eye = jnp.eye(S5_GROUP, dtype=F32)
    dcol = (d_skip.reshape(groups, S5_GROUP, 1) * eye[None]).reshape(groups, S5_GROUP * S5_GROUP, 1)
    return (row(lam_re), row(lam_im), row(ls), bt(b_re), bt(b_im), both(c_re), both(c_im), dcol)


def kernel(x, c, w_ada, b_ada, norm_mix, w_in, s5_lambda_re, s5_lambda_im, s5_log_step,
           s5_b_re, s5_b_im, s5_c_re, s5_c_im, s5_d, w_s5_glu, w_branch_s5, w_branch_fnet,
           w_out, norm_ffn, w_ffn_in, w_ffn_out, norm_final):
    batch, seq, d = x.shape
    t = batch * seq
    w5 = s5_d.shape[-1]
    wf = w_branch_fnet.shape[1]
    n2 = FFT_N2
    n1 = seq // n2
    assert w_in.shape[0] == 1, "multi-layer stacks are not supported"
    assert seq % FFN_ROW_TILE == 0 and n1 % BF16_ROWS == 0

    dft = jnp.asarray(_chan_dft_table()).astype(BF16)
    perm = jnp.asarray(_perm_table(ROW_TILE))
    f1 = jnp.asarray(_fft1_table(n1, seq)).astype(BF16)
    m2 = jnp.asarray(_fft2_table(n1, n2)).astype(BF16)

    xs = x.reshape(t, d).astype(F32)
    ct = c.astype(F32).T
    for l in range(w_in.shape[0]):
        mod = _ada(ct, w_ada[l], b_ada[l])
        sh_m, sc_m, g_m, sh_f, sc_f, g_f = [m.reshape(batch, 1, d)
                                            for m in jnp.split(mod, N_MOD, axis=-1)]
        wl = w_in[l]
        us5, v, gs, gf = _inproj(
            xs, norm_mix[l].reshape(1, d), sc_m, sh_m,
            wl[:, :w5].T.astype(BF16), wl[:, w5:w5 + wf].astype(BF16),
            wl[:, w5 + wf:].astype(BF16), perm, dft, batch, seq)

        s5p = _s5_param_layout(s5_lambda_re[l], s5_lambda_im[l], s5_log_step[l], s5_b_re[l],
                               s5_b_im[l], s5_c_re[l], s5_c_im[l], s5_d[l])
        ys5 = _s5(us5.reshape(t // CHUNK, w5 * CHUNK), *s5p, chunks_per_seq=seq // CHUNK)

        yfn = _fft(f1, m2, v)

        xs, h2 = _merge(
            xs, ys5.reshape(t // CHUNK, w5, CHUNK), yfn.reshape(t, wf), gs, gf, g_m,
            norm_ffn[l].reshape(1, d), sc_f, sh_f, w_s5_glu[l].astype(BF16),
            w_branch_s5[l].astype(BF16), w_branch_fnet[l].astype(BF16), w_out[l].astype(BF16), seq)
        xs = _ffn(h2, w_ffn_in[l].astype(BF16), w_ffn_out[l].astype(BF16), xs, g_f,
                  norm_final.reshape(1, d), seq)
    return xs.reshape(batch, seq, d).astype(x.dtype)
```

```python
import functools
import math

import jax
import jax.numpy as jnp
import numpy as np
from jax import lax
from jax.experimental import pallas as pl
from jax.experimental.pallas import tpu as pltpu

F32 = jnp.float32
BF16 = jnp.bfloat16

S5_GROUP = 16
S5_STATE = 64
FNET_GROUP = 256
N_MOD = 6
EPS = 1e-6

LANES = 128
SUBLANES = 8
BF16_ROWS = 16
VMEM_LIMIT_BYTES = 60 * 1024 * 1024

CHUNK = LANES

ROW_TILE = 512
FFN_ROW_TILE = 1024
FFN_TILE = 512
ADA_TILE = 1024

FFT_GROUP = SUBLANES
FFT_N2 = ROW_TILE // FFT_GROUP


def _params(*semantics):
    return pltpu.CompilerParams(dimension_semantics=semantics, vmem_limit_bytes=VMEM_LIMIT_BYTES)


def _const_spec(shape):
    zeros = (0,) * len(shape)
    return pl.BlockSpec(shape, lambda *_: zeros, pipeline_mode=pl.Buffered(1))


def _dot(a, b):
    return jnp.dot(a, b, preferred_element_type=F32)


def _dot_nt(a, b, precision=None):
    return lax.dot_general(a, b, (((1,), (1,)), ((), ())), precision=precision,
                           preferred_element_type=F32)


def _rms_modulate(x, gain, scale, shift):
    ms = jnp.mean(x * x, axis=-1, keepdims=True)
    y = x * lax.rsqrt(ms + EPS)
    return (y * gain) * (1.0 + scale) + shift


def _ada_kernel(ct_ref, w_ref, b_ref, o_ref, cb_scr):
    d, nb = ct_ref.shape
    tn = o_ref.shape[1]

    @pl.when(pl.program_id(0) == 0)
    def _():
        cs = ct_ref[...]
        cs = cs * jax.nn.sigmoid(cs)
        for b in range(nb):
            cb_scr[b] = jnp.broadcast_to(cs[:, b:b + 1], (d, LANES))

    def body(k, accs):
        k0 = pl.multiple_of(k * SUBLANES, SUBLANES)
        w = w_ref[pl.ds(k0, SUBLANES), :]
        return tuple(acc + w * jnp.tile(cb_scr[b, pl.ds(k0, SUBLANES), :], (1, tn // LANES))
                     for b, acc in enumerate(accs))

    init = tuple(jnp.zeros((SUBLANES, tn), F32) for _ in range(nb))
    accs = lax.fori_loop(0, d // SUBLANES, body, init, unroll=8)
    for b, acc in enumerate(accs):
        o_ref[b:b + 1, :] = jnp.sum(acc, axis=0, keepdims=True) + b_ref[...]


def _ada(ct, w, bias):
    d, nb = ct.shape
    n = w.shape[1]
    return pl.pallas_call(
        _ada_kernel,
        grid=(n // ADA_TILE,),
        in_specs=[pl.BlockSpec((d, nb), lambda j: (0, 0)),
                  pl.BlockSpec((d, ADA_TILE), lambda j: (0, j)),
                  pl.BlockSpec((1, ADA_TILE), lambda j: (0, j))],
        out_specs=pl.BlockSpec((nb, ADA_TILE), lambda j: (0, j)),
        out_shape=jax.ShapeDtypeStruct((nb, n), F32),
        scratch_shapes=[pltpu.VMEM((nb, d, LANES), F32)],
        compiler_params=_params("arbitrary"),
        name="ada",
    )(ct, w, bias.reshape(1, n))


def _inproj_kernel(x_ref, g_ref, sc_ref, sh_ref, ws5t_ref, wfn_ref, wg_ref, perm_ref, dft_ref,
                   us5_ref, v_ref, gs_ref, gf_ref):
    tm, d = x_ref.shape
    hb = _rms_modulate(x_ref[...], g_ref[...], sc_ref[...], sh_ref[...]).astype(BF16)

    ut = _dot_nt(ws5t_ref[...], hb)
    for ci in range(tm // CHUNK):
        us5_ref[ci] = ut[:, ci * CHUNK:(ci + 1) * CHUNK].astype(BF16)

    uf = _dot(hb, wfn_ref[...]).astype(BF16)
    ufp = _dot(perm_ref[...], uf).astype(BF16)
    n2 = tm // FFT_GROUP
    for g in range(uf.shape[1] // FNET_GROUP):
        cols = slice(g * FNET_GROUP, (g + 1) * FNET_GROUP)
        pq = _dot(ufp[:, cols], dft_ref[...])
        re = pq[:, :FNET_GROUP].reshape(n2, FFT_GROUP, FNET_GROUP)
        im = pq[:, FNET_GROUP:].reshape(n2, FFT_GROUP, FNET_GROUP)
        v_ref[:, :, cols] = jnp.concatenate([re, im], axis=1).astype(BF16)

    nb = 512
    for n0 in range(0, 2 * d, nb):
        s = jax.nn.sigmoid(_dot(hb, wg_ref[:, n0:n0 + nb])).astype(BF16)
        if n0 < d:
            gs_ref[:, n0:n0 + nb] = s
        else:
            gf_ref[:, n0 - d:n0 - d + nb] = s


def _inproj(x2d, gain, scale, shift, ws5t, wfn, wg, perm, dft, batch, seq):
    t, d = x2d.shape
    tm = ROW_TILE
    bps = seq // tm
    w5 = ws5t.shape[0]
    wf = wfn.shape[1]
    vrows = 2 * FFT_GROUP
    per_batch = pl.BlockSpec((None, 1, d), lambda i: (i // bps, 0, 0))
    return pl.pallas_call(
        _inproj_kernel,
        grid=(t // tm,),
        in_specs=[pl.BlockSpec((tm, d), lambda i: (i, 0)),
                  _const_spec((1, d)), per_batch, per_batch,
                  _const_spec(ws5t.shape), _const_spec(wfn.shape), _const_spec(wg.shape),
                  _const_spec(perm.shape), _const_spec(dft.shape)],
        out_specs=[pl.BlockSpec((tm // CHUNK, w5, CHUNK), lambda i: (i, 0, 0)),
                   pl.BlockSpec((None, FFT_N2, vrows, wf), lambda i: (i // bps, 0, i % bps, 0)),
                   pl.BlockSpec((tm, d), lambda i: (i, 0)),
                   pl.BlockSpec((tm, d), lambda i: (i, 0))],
        out_shape=[jax.ShapeDtypeStruct((t // CHUNK, w5, CHUNK), BF16),
                   jax.ShapeDtypeStruct((batch, FFT_N2, bps * vrows, wf), BF16),
                   jax.ShapeDtypeStruct((t, d), BF16),
                   jax.ShapeDtypeStruct((t, d), BF16)],
        compiler_params=_params("arbitrary"),
        name="inproj",
    )(x2d, gain, scale, shift, ws5t, wfn, wg, perm, dft)


def _s5_kernel(x_ref, lre_ref, lim_ref, ls_ref, btr_ref, bti_ref, cr_ref, ci_ref, dcol_ref,
               y_ref, v_scr, w_scr, win_scr, wout_scr, amul_scr, ys_scr, *, chunks_per_seq):
    nc = x_ref.shape[0]
    ns = 2 * S5_STATE
    hg = S5_GROUP
    tiles = CHUNK // SUBLANES
    lane = lax.broadcasted_iota(jnp.int32, (1, ns), 1)
    fwd = lane < S5_STATE

    step = pl.program_id(0)
    cur = step % 2
    prev = 1 - cur

    @pl.when(step == 0)
    def _():
        w_scr[1] = jnp.zeros(w_scr.shape[1:], BF16)
        win_scr[1] = jnp.zeros(win_scr.shape[1:], BF16)
        wout_scr[1] = jnp.zeros(wout_scr.shape[1:], BF16)
        amul_scr[1] = jnp.zeros(amul_scr.shape[1:], F32)

    lam_re = lre_ref[...]
    lam_im = lim_ref[...]
    dt = jnp.exp(ls_ref[...])
    lr = lam_re * dt
    li = lam_im * dt

    def powers(e):
        mag = jnp.exp(e * lr)
        return mag * jnp.cos(e * li), mag * jnp.sin(e * li)

    def cmul(ar, ai, br, bi):
        return ar * br - ai * bi, ar * bi + ai * br

    k_idx = lax.broadcasted_iota(jnp.int32, (3 * SUBLANES, ns), 0)
    base_r, base_i = powers((SUBLANES * k_idx).astype(F32))
    l_idx = lax.broadcasted_iota(jnp.int32, (SUBLANES, ns), 0)

    def base(k_fwd, k_bwd):
        rows = lambda t, k: jnp.broadcast_to(t[k:k + 1], (SUBLANES, ns))
        return (jnp.where(fwd, rows(base_r, k_fwd), rows(base_r, k_bwd)),
                jnp.where(fwd, rows(base_i, k_fwd), rows(base_i, k_bwd)))

    lb_re, lb_im = powers(jnp.ones((1, ns), F32))
    den = lam_re * lam_re + lam_im * lam_im
    num_re = lb_re - 1.0
    num_im = lb_im
    coef_re = (num_re * lam_re + num_im * lam_im) / den
    coef_im = (num_im * lam_re - num_re * lam_im) / den
    btr = btr_ref[...]
    bti = bti_ref[...]
    bbr = coef_re * btr - coef_im * bti
    bbi = coef_re * bti + coef_im * btr
    cr = cr_ref[...]
    ci = ci_ref[...]

    qr = jnp.concatenate([cr[h:h + 1] * bbr - ci[h:h + 1] * bbi for h in range(hg)], axis=0)
    qi = jnp.concatenate([cr[h:h + 1] * bbi + ci[h:h + 1] * bbr for h in range(hg)], axis=0)

    up_r, up_i = powers(l_idx.astype(F32))
    dn_r, dn_i = powers((SUBLANES - l_idx).astype(F32))
    fwd_tile = lax.broadcasted_iota(jnp.int32, (SUBLANES, ns), 1) < S5_STATE
    pr_tiles, pi_tiles = [], []
    for a in range(2 * tiles):
        if a < tiles:
            tr, ti = cmul(*base(tiles - 1 - a, tiles - 1 - a), dn_r, dn_i)
            keep = jnp.logical_not(fwd_tile)
        else:
            tr, ti = cmul(*base(a - tiles, a - tiles), up_r, up_i)
            keep = (fwd_tile | (l_idx == 0)) if a == tiles else fwd_tile
        pr_tiles.append(jnp.where(keep, tr, 0.0))
        pi_tiles.append(jnp.where(keep, ti, 0.0))
    pr = jnp.concatenate(pr_tiles, axis=0)
    pi = jnp.concatenate(pi_tiles, axis=0)
    hi = lax.Precision.HIGHEST
    v = _dot_nt(qr, pr, hi) - _dot_nt(qi, pi, hi)
    centre = lax.broadcasted_iota(jnp.int32, (1, 2 * CHUNK), 1) == CHUNK
    v_scr[...] = v + jnp.where(centre, dcol_ref[...], 0.0)

    def affine_table(k_fwd, k_bwd, ur, ui):
        tr, ti = zip(*[cmul(*base(k_fwd(a), k_bwd(a)), ur, ui) for a in range(tiles)])
        return jnp.concatenate(tr, axis=0), jnp.concatenate(ti, axis=0)

    ur, ui = powers(jnp.where(fwd, SUBLANES - 1 - l_idx, l_idx).astype(F32))
    pin_r, pin_i = affine_table(lambda a: tiles - 1 - a, lambda a: a, ur, ui)
    for hq in range(hg):
        rows = slice(hq * CHUNK, (hq + 1) * CHUNK)
        br = bbr[hq:hq + 1]
        bi = bbi[hq:hq + 1]
        win_scr[cur, rows, :ns] = (pin_r * br - pin_i * bi).astype(BF16)
        win_scr[cur, rows, ns:] = (pin_r * bi + pin_i * br).astype(BF16)

    ur, ui = powers(jnp.where(fwd, l_idx + 1, SUBLANES - l_idx).astype(F32))
    po_r, po_i = affine_table(lambda a: a, lambda a: tiles - 1 - a, ur, ui)
    for h in range(hg):
        rows = slice(h * CHUNK, (h + 1) * CHUNK)
        c_r = cr[h:h + 1]
        c_i = ci[h:h + 1]
        wout_scr[cur, rows, :ns] = (c_r * po_r - c_i * po_i).astype(BF16)
        wout_scr[cur, rows, ns:] = (-(c_r * po_i + c_i * po_r)).astype(BF16)

    ar, ai = base_r[tiles:tiles + 1], base_i[tiles:tiles + 1]
    dist = 1
    while dist < chunks_per_seq:
        row = 2 * (dist.bit_length() - 1)
        amul_scr[cur, row:row + 1] = ar
        amul_scr[cur, row + 1:row + 2] = ai
        ar, ai = cmul(ar, ai, ar, ai)
        dist *= 2

    x = x_ref[...]
    z = _dot(x, win_scr[prev])
    zr = z[:, :ns]
    zi = z[:, ns:]

    pos = lax.broadcasted_iota(jnp.int32, (nc, ns), 0) & (chunks_per_seq - 1)

    def in_seq(dist):
        return ((pos >= jnp.where(fwd, dist, 0))
                & (pos < jnp.where(fwd, chunks_per_seq, chunks_per_seq - dist)))

    def neighbour(a, dist):
        return jnp.where(fwd, pltpu.roll(a, dist, 0), pltpu.roll(a, nc - dist, 0))

    dist = 1
    while dist < chunks_per_seq:
        row = 2 * (dist.bit_length() - 1)
        ar = amul_scr[prev, row:row + 1]
        ai = amul_scr[prev, row + 1:row + 2]
        nr = neighbour(zr, dist)
        ni = neighbour(zi, dist)
        ok = in_seq(dist)
        zr, zi = (zr + jnp.where(ok, ar * nr - ai * ni, 0.0),
                  zi + jnp.where(ok, ar * ni + ai * nr, 0.0))
        dist *= 2
    ok = in_seq(1)
    sr = jnp.where(ok, neighbour(zr, 1), 0.0)
    si = jnp.where(ok, neighbour(zi, 1), 0.0)
    s_in = jnp.concatenate([sr, si], axis=1).astype(BF16)
    ys_scr[...] = _dot_nt(s_in, wout_scr[prev])

    wide = 2 * CHUNK
    for hp in range(hg // 2):
        for hl in range(2):
            for hq in range(hg):
                r = (2 * hp + hl) * hg + hq
                row = jnp.broadcast_to(v_scr[r:r + 1, :], (CHUNK, 2 * CHUNK))
                blk = pltpu.roll(row, CHUNK, 1, stride=1, stride_axis=0)
                w_scr[cur, hp, hq * CHUNK:(hq + 1) * CHUNK, hl * CHUNK:(hl + 1) * CHUNK] = (
                    blk[:, :CHUNK].astype(BF16))
        cols = slice(hp * wide, (hp + 1) * wide)
        y_ref[:, cols] = _dot(x_ref[...], w_scr[prev, hp]) + ys_scr[:, cols]


def _s5(us5, lre, lim, ls, btr, bti, cr, ci, dcol, chunks_per_seq):
    nc, width = us5.shape
    groups = lre.shape[0]
    gw = S5_GROUP * CHUNK
    ns = 2 * S5_STATE
    assert chunks_per_seq & (chunks_per_seq - 1) == 0
    built = lambda s: jnp.minimum(s, groups - 1)
    applied = lambda s: jnp.maximum(s - 1, 0)
    row = pl.BlockSpec((None, 1, ns), lambda s: (built(s), 0, 0))
    mat = pl.BlockSpec((None, S5_GROUP, ns), lambda s: (built(s), 0, 0))
    return pl.pallas_call(
        functools.partial(_s5_kernel, chunks_per_seq=chunks_per_seq),
        grid=(groups + 1,),
        in_specs=[pl.BlockSpec((nc, gw), lambda s: (0, applied(s))),
                  row, row, row, mat, mat, mat, mat,
                  pl.BlockSpec((None, S5_GROUP * S5_GROUP, 1), lambda s: (built(s), 0, 0))],
        out_specs=pl.BlockSpec((nc, gw), lambda s: (0, applied(s))),
        out_shape=jax.ShapeDtypeStruct((nc, width), F32),
        scratch_shapes=[pltpu.VMEM((S5_GROUP * S5_GROUP, 2 * CHUNK), F32),
                        pltpu.VMEM((2, S5_GROUP // 2, gw, 2 * CHUNK), BF16),
                        pltpu.VMEM((2, gw, 2 * ns), BF16),
                        pltpu.VMEM((2, gw, 2 * ns), BF16),
                        pltpu.VMEM((2, 2 * SUBLANES, ns), F32),
                        pltpu.VMEM((nc, gw), F32)],
        compiler_params=_params("arbitrary"),
        name="s5",
    )(us5, lre, lim, ls, btr, bti, cr, ci, dcol)


def _fft_kernel(f_ref, m_ref, v_ref, o_ref, a_scr):
    nq, n2, tile, c = a_scr.shape

    def stage1(j, carry):
        res = _dot(f_ref[...], v_ref[j]).astype(BF16)
        for q in range(nq):
            a_scr[q, j] = res[q * tile:(q + 1) * tile]
        return carry

    lax.fori_loop(0, n2, stage1, 0, unroll=8)

    per = tile // FFT_GROUP
    for k in range(nq // per):
        parts = []
        for qq in range(per):
            q = k * per + qq
            rhs = a_scr[q].reshape(n2 * tile, c)
            parts.append(_dot(m_ref[q], rhs).reshape(n2, FFT_GROUP, c))
        o_ref[:, k * tile:(k + 1) * tile, :] = jnp.concatenate(parts, axis=1).astype(BF16)


def _fft(f1, m2, v):
    batch, n2, rows, c = v.shape
    tile = 2 * FFT_GROUP
    cs = FNET_GROUP
    return pl.pallas_call(
        _fft_kernel,
        grid=(batch, c // cs),
        in_specs=[_const_spec(f1.shape), _const_spec(m2.shape),
                  pl.BlockSpec((None, n2, rows, cs), lambda b, j: (b, 0, 0, j))],
        out_specs=pl.BlockSpec((None, n2, rows // 2, cs), lambda b, j: (b, 0, 0, j)),
        out_shape=jax.ShapeDtypeStruct((batch, n2, rows // 2, c), BF16),
        scratch_shapes=[pltpu.VMEM((rows // tile, n2, tile, cs), BF16)],
        compiler_params=_params("arbitrary", "arbitrary"),
        name="fft",
    )(f1, m2, v)


@functools.lru_cache(maxsize=None)
def _chan_dft_table():
    n = FNET_GROUP
    idx = np.arange(n)
    ang = 2.0 * np.pi * ((idx[:, None] * idx[None, :]) % n) / n
    return (np.concatenate([np.cos(ang), -np.sin(ang)], axis=1) / math.sqrt(n)).astype(np.float32)


@functools.lru_cache(maxsize=None)
def _perm_table(tm):
    n2 = tm // FFT_GROUP
    dst = np.arange(tm)
    p = np.zeros((tm, tm), np.float32)
    p[dst, (dst % FFT_GROUP) * n2 + dst // FFT_GROUP] = 1.0
    return p.astype(BF16)


@functools.lru_cache(maxsize=None)
def _fft1_table(n1, seq):
    g = FFT_GROUP
    idx = np.arange(n1)
    ang = 2.0 * np.pi * ((idx[:, None] * idx[None, :]) % n1) / n1
    c, s = np.cos(ang), np.sin(ang)
    blocks = np.array([[c, s], [-s, c]])
    t = blocks.reshape(2, 2, n1 // g, g, n1 // g, g).transpose(2, 0, 3, 4, 1, 5)
    return (t.reshape(2 * n1, 2 * n1) / math.sqrt(seq)).astype(np.float32)


@functools.lru_cache(maxsize=None)
def _fft2_table(n1, n2):
    g = FFT_GROUP
    n = n1 * n2
    q = np.arange(n1 // g)[:, None, None, None]
    k2 = np.arange(n2)[None, :, None, None]
    k1l = np.arange(g)[None, None, :, None]
    nn = np.arange(n2)[None, None, None, :]
    ang = 2.0 * np.pi * ((nn * (g * q + k1l + n1 * k2)) % n) / n
    cs = np.stack([np.cos(ang), np.sin(ang)], axis=-1).astype(np.float32)
    m = np.zeros((n1 // g, n2, g, n2, 2, g), np.float32)
    for l in range(g):
        m[:, :, l, :, :, l] = cs[:, :, l]
    return m.reshape(n1 // g, n2 * g, n2 * 2 * g)


def _merge_kernel(x_ref, ys5_ref, yfn_ref, gs_ref, gf_ref, gm_ref, g2_ref, sc_ref, sh_ref,
                  wglu_ref, wbs_ref, wbf_ref, wo_ref, x1_ref, h2_ref, m_scr):
    tm, d = x_ref.shape
    w5 = wbs_ref.shape[0]
    parts = [jax.nn.gelu(ys5_ref[ci].T, approximate=True).astype(BF16)
             for ci in range(tm // CHUNK)]
    ge = jnp.concatenate(parts, axis=0)
    z = _dot(ge, wglu_ref[...])
    ys = (z[:, :w5] * jax.nn.sigmoid(z[:, w5:])).astype(BF16)
    yfn = yfn_ref[...]
    nb = 512
    for n0 in range(0, d, nb):
        cols = slice(n0, n0 + nb)
        m_scr[:, cols] = (gs_ref[:, cols] * _dot(ys, wbs_ref[:, cols]).astype(BF16)
                          + gf_ref[:, cols] * _dot(yfn, wbf_ref[:, cols]).astype(BF16))
    rg = 256
    for r0 in range(0, tm, rg):
        rows = slice(r0, r0 + rg)
        x1 = x_ref[rows, :] + gm_ref[...] * _dot(m_scr[rows, :], wo_ref[...])
        x1_ref[rows, :] = x1
        h2_ref[rows, :] = _rms_modulate(x1, g2_ref[...], sc_ref[...], sh_ref[...]).astype(BF16)


def _merge(x2d, ys5, yfn, gs, gf, gm, g2, scale, shift, wglu, wbs, wbf, wo, seq):
    t, d = x2d.shape
    tm = ROW_TILE
    bps = seq // tm
    w5 = ys5.shape[1]
    wf = yfn.shape[1]
    rows = lambda width: pl.BlockSpec((tm, width), lambda i: (i, 0))
    per_batch = pl.BlockSpec((None, 1, d), lambda i: (i // bps, 0, 0))
    return pl.pallas_call(
        _merge_kernel,
        grid=(t // tm,),
        in_specs=[rows(d),
                  pl.BlockSpec((tm // CHUNK, w5, CHUNK), lambda i: (i, 0, 0)),
                  rows(wf), rows(d), rows(d),
                  per_batch, _const_spec((1, d)), per_batch, per_batch,
                  _const_spec(wglu.shape), _const_spec(wbs.shape), _const_spec(wbf.shape),
                  _const_spec(wo.shape)],
        out_specs=[rows(d), rows(d)],
        out_shape=[jax.ShapeDtypeStruct((t, d), F32), jax.ShapeDtypeStruct((t, d), BF16)],
        scratch_shapes=[pltpu.VMEM((tm, d), BF16)],
        compiler_params=_params("arbitrary"),
        name="merge",
    )(x2d, ys5, yfn, gs, gf, gm, g2, scale, shift, wglu, wbs, wbf, wo)


def _ffn_kernel(h_ref, wa_ref, wb_ref, wo_ref, x1_hbm, gate_ref, gn_ref, o_ref, x1_buf, x1_sem):
    i = pl.program_id(0)
    f = pl.program_id(1)
    tm, d = o_ref.shape

    def x1_copy():
        return pltpu.make_async_copy(x1_hbm.at[pl.ds(i * tm, tm), :], x1_buf, x1_sem)

    def hidden(rows):
        h = h_ref[rows, :]
        a = _dot(h, wa_ref[...])
        b = _dot(h, wb_ref[...])
        return (a * jax.nn.sigmoid(a) * b).astype(BF16)

    nb = 512
    last = pl.num_programs(1) - 1

    @pl.when(f == 0)
    def _():
        x1_copy().start()
        act = hidden(slice(None))
        for n0 in range(0, d, nb):
            o_ref[:, n0:n0 + nb] = _dot(act, wo_ref[:, n0:n0 + nb])

    @pl.when((f > 0) & (f < last))
    def _():
        act = hidden(slice(None))
        for n0 in range(0, d, nb):
            o_ref[:, n0:n0 + nb] += _dot(act, wo_ref[:, n0:n0 + nb])

    @pl.when(f == last)
    def _():
        x1_copy().wait()
        rg = 256
        for r0 in range(0, tm, rg):
            rows = slice(r0, r0 + rg)
            acc = o_ref[rows, :] + _dot(hidden(rows), wo_ref[...])
            x2 = x1_buf[rows, :] + gate_ref[...] * acc
            ms = jnp.mean(x2 * x2, axis=-1, keepdims=True)
            o_ref[rows, :] = (x2 * lax.rsqrt(ms + EPS)) * gn_ref[...]


def _ffn(h2, w_in, w_out, x1, gate, gn, seq):
    t, d = h2.shape
    ff = w_out.shape[0]
    tm, tf = FFN_ROW_TILE, FFN_TILE
    bps = seq // tm
    nf = ff // tf
    assert nf >= 2
    return pl.pallas_call(
        _ffn_kernel,
        grid=(t // tm, nf),
        in_specs=[pl.BlockSpec((tm, d), lambda i, f: (i, 0)),
                  pl.BlockSpec((d, tf), lambda i, f: (0, f)),
                  pl.BlockSpec((d, tf), lambda i, f: (0, f + nf)),
                  pl.BlockSpec((tf, d), lambda i, f: (f, 0)),
                  pl.BlockSpec(memory_space=pl.ANY),
                  pl.BlockSpec((None, 1, d), lambda i, f: (i // bps, 0, 0)),
                  pl.BlockSpec((1, d), lambda i, f: (0, 0))],
        out_specs=pl.BlockSpec((tm, d), lambda i, f: (i, 0)),
        out_shape=jax.ShapeDtypeStruct((t, d), F32),
        scratch_shapes=[pltpu.VMEM((tm, d), F32), pltpu.SemaphoreType.DMA],
        compiler_params=_params("arbitrary", "arbitrary"),
        name="ffn",
    )(h2, w_in, w_in, w_out, x1, gate, gn)


def _s5_param_layout(lam_re, lam_im, log_step, b_re, b_im, c_re, c_im, d_skip):
    groups = lam_re.shape[1]
    both = lambda p: jnp.concatenate([p[0], p[1]], axis=-1)
    row = lambda p: both(p)[:, None, :]
    ls = jnp.broadcast_to(log_step[:, :, None], log_step.shape + (S5_STATE,))
    bt = lambda p: both(jnp.swapaxes(p, -1, -2))
    eye = jnp.eye(S5_GROUP, dtype=F32)
    dcol = (d_skip.reshape(groups, S5_GROUP, 1) * eye[None]).reshape(groups, S5_GROUP * S5_GROUP, 1)
    return (row(lam_re), row(lam_im), row(ls), bt(b_re), bt(b_im), both(c_re), both(c_im), dcol)


def kernel(x, c, w_ada, b_ada, norm_mix, w_in, s5_lambda_re, s5_lambda_im, s5_log_step,
           s5_b_re, s5_b_im, s5_c_re, s5_c_im, s5_d, w_s5_glu, w_branch_s5, w_branch_fnet,
           w_out, norm_ffn, w_ffn_in, w_ffn_out, norm_final):
    batch, seq, d = x.shape
    t = batch * seq
    w5 = s5_d.shape[-1]
    wf = w_branch_fnet.shape[1]
    n2 = FFT_N2
    n1 = seq // n2
    assert w_in.shape[0] == 1, "multi-layer stacks are not supported"
    assert seq % FFN_ROW_TILE == 0 and n1 % BF16_ROWS == 0

    dft = jnp.asarray(_chan_dft_table()).astype(BF16)
    perm = jnp.asarray(_perm_table(ROW_TILE))
    f1 = jnp.asarray(_fft1_table(n1, seq)).astype(BF16)
    m2 = jnp.asarray(_fft2_table(n1, n2)).astype(BF16)

    xs = x.reshape(t, d).astype(F32)
    ct = c.astype(F32).T
    for l in range(w_in.shape[0]):
        mod = _ada(ct, w_ada[l], b_ada[l])
        sh_m, sc_m, g_m, sh_f, sc_f, g_f = [m.reshape(batch, 1, d)
                                            for m in jnp.split(mod, N_MOD, axis=-1)]
        wl = w_in[l]
        us5, v, gs, gf = _inproj(
            xs, norm_mix[l].reshape(1, d), sc_m, sh_m,
            wl[:, :w5].T.astype(BF16), wl[:, w5:w5 + wf].astype(BF16),
            wl[:, w5 + wf:].astype(BF16), perm, dft, batch, seq)

        s5p = _s5_param_layout(s5_lambda_re[l], s5_lambda_im[l], s5_log_step[l], s5_b_re[l],
                               s5_b_im[l], s5_c_re[l], s5_c_im[l], s5_d[l])
        ys5 = _s5(us5.reshape(t // CHUNK, w5 * CHUNK), *s5p, chunks_per_seq=seq // CHUNK)

        yfn = _fft(f1, m2, v)

        xs, h2 = _merge(
            xs, ys5.reshape(t // CHUNK, w5, CHUNK), yfn.reshape(t, wf), gs, gf, g_m,
            norm_ffn[l].reshape(1, d), sc_f, sh_f, w_s5_glu[l].astype(BF16),
            w_branch_s5[l].astype(BF16), w_branch_fnet[l].astype(BF16), w_out[l].astype(BF16), seq)
        xs = _ffn(h2, w_ffn_in[l].astype(BF16), w_ffn_out[l].astype(BF16), xs, g_f,
                  norm_final.reshape(1, d), seq)
    return xs.reshape(batch, seq, d).astype(x.dtype)
```

```python
import functools
import math

import jax
import jax.numpy as jnp
import numpy as np
from jax import lax
from jax.experimental import pallas as pl
from jax.experimental.pallas import tpu as pltpu

F32 = jnp.float32
BF16 = jnp.bfloat16

S5_GROUP = 16
S5_STATE = 64
FNET_GROUP = 256
N_MOD = 6
EPS = 1e-6

LANES = 128
SUBLANES = 8
BF16_ROWS = 16
VMEM_LIMIT_BYTES = 60 * 1024 * 1024

CHUNK = LANES

ROW_TILE = 512
FFN_ROW_TILE = 1024
FFN_TILE = 512
ADA_TILE = 1024

FFT_GROUP = SUBLANES
FFT_N2 = ROW_TILE // FFT_GROUP


def _params(*semantics):
    return pltpu.CompilerParams(dimension_semantics=semantics, vmem_limit_bytes=VMEM_LIMIT_BYTES)


def _const_spec(shape):
    zeros = (0,) * len(shape)
    return pl.BlockSpec(shape, lambda *_: zeros, pipeline_mode=pl.Buffered(1))


def _dot(a, b):
    return jnp.dot(a, b, preferred_element_type=F32)


def _dot_nt(a, b, precision=None):
    return lax.dot_general(a, b, (((1,), (1,)), ((), ())), precision=precision,
                           preferred_element_type=F32)


def _rms_modulate(x, gain, scale, shift):
    ms = jnp.mean(x * x, axis=-1, keepdims=True)
    y = x * lax.rsqrt(ms + EPS)
    return (y * gain) * (1.0 + scale) + shift


def _ada_kernel(ct_ref, w_ref, b_ref, o_ref, cb_scr):
    d, nb = ct_ref.shape
    tn = o_ref.shape[1]

    @pl.when(pl.program_id(0) == 0)
    def _():
        cs = ct_ref[...]
        cs = cs * jax.nn.sigmoid(cs)
        for b in range(nb):
            cb_scr[b] = jnp.broadcast_to(cs[:, b:b + 1], (d, LANES))

    def body(k, accs):
        k0 = pl.multiple_of(k * SUBLANES, SUBLANES)
        w = w_ref[pl.ds(k0, SUBLANES), :]
        return tuple(acc + w * jnp.tile(cb_scr[b, pl.ds(k0, SUBLANES), :], (1, tn // LANES))
                     for b, acc in enumerate(accs))

    init = tuple(jnp.zeros((SUBLANES, tn), F32) for _ in range(nb))
    accs = lax.fori_loop(0, d // SUBLANES, body, init, unroll=8)
    for b, acc in enumerate(accs):
        o_ref[b:b + 1, :] = jnp.sum(acc, axis=0, keepdims=True) + b_ref[...]


def _ada(ct, w, bias):
    d, nb = ct.shape
    n = w.shape[1]
    return pl.pallas_call(
        _ada_kernel,
        grid=(n // ADA_TILE,),
        in_specs=[pl.BlockSpec((d, nb), lambda j: (0, 0)),
                  pl.BlockSpec((d, ADA_TILE), lambda j: (0, j)),
                  pl.BlockSpec((1, ADA_TILE), lambda j: (0, j))],
        out_specs=pl.BlockSpec((nb, ADA_TILE), lambda j: (0, j)),
        out_shape=jax.ShapeDtypeStruct((nb, n), F32),
        scratch_shapes=[pltpu.VMEM((nb, d, LANES), F32)],
        compiler_params=_params("arbitrary"),
        name="ada",
    )(ct, w, bias.reshape(1, n))


def _inproj_kernel(x_ref, g_ref, sc_ref, sh_ref, ws5t_ref, wfn_ref, wg_ref, perm_ref, dft_ref,
                   us5_ref, v_ref, gs_ref, gf_ref):
    tm, d = x_ref.shape
    hb = _rms_modulate(x_ref[...], g_ref[...], sc_ref[...], sh_ref[...]).astype(BF16)

    ut = _dot_nt(ws5t_ref[...], hb)
    for ci in range(tm // CHUNK):
        us5_ref[ci] = ut[:, ci * CHUNK:(ci + 1) * CHUNK].astype(BF16)

    uf = _dot(hb, wfn_ref[...]).astype(BF16)
    ufp = _dot(perm_ref[...], uf).astype(BF16)
    n2 = tm // FFT_GROUP
    for g in range(uf.shape[1] // FNET_GROUP):
        cols = slice(g * FNET_GROUP, (g + 1) * FNET_GROUP)
        pq = _dot(ufp[:, cols], dft_ref[...])
        re = pq[:, :FNET_GROUP].reshape(n2, FFT_GROUP, FNET_GROUP)
        im = pq[:, FNET_GROUP:].reshape(n2, FFT_GROUP, FNET_GROUP)
        v_ref[g] = jnp.concatenate([re, im], axis=1).astype(BF16)

    nb = 512
    for n0 in range(0, 2 * d, nb):
        s = jax.nn.sigmoid(_dot(hb, wg_ref[:, n0:n0 + nb])).astype(BF16)
        if n0 < d:
            gs_ref[:, n0:n0 + nb] = s
        else:
            gf_ref[:, n0 - d:n0 - d + nb] = s


def _inproj(x2d, gain, scale, shift, ws5t, wfn, wg, perm, dft, batch, seq):
    t, d = x2d.shape
    tm = ROW_TILE
    bps = seq // tm
    w5 = ws5t.shape[0]
    wf = wfn.shape[1]
    vrows = 2 * FFT_GROUP
    per_batch = pl.BlockSpec((None, 1, d), lambda i: (i // bps, 0, 0))
    return pl.pallas_call(
        _inproj_kernel,
        grid=(t // tm,),
        in_specs=[pl.BlockSpec((tm, d), lambda i: (i, 0)),
                  _const_spec((1, d)), per_batch, per_batch,
                  _const_spec(ws5t.shape), _const_spec(wfn.shape), _const_spec(wg.shape),
                  _const_spec(perm.shape), _const_spec(dft.shape)],
        out_specs=[pl.BlockSpec((tm // CHUNK, w5, CHUNK), lambda i: (i, 0, 0)),
                   pl.BlockSpec((None, wf // FNET_GROUP, FFT_N2, vrows, FNET_GROUP),
                                lambda i: (i // bps, 0, 0, i % bps, 0)),
                   pl.BlockSpec((tm, d), lambda i: (i, 0)),
                   pl.BlockSpec((tm, d), lambda i: (i, 0))],
        out_shape=[jax.ShapeDtypeStruct((t // CHUNK, w5, CHUNK), BF16),
                   jax.ShapeDtypeStruct((batch, wf // FNET_GROUP, FFT_N2, bps * vrows, FNET_GROUP),
                                        BF16),
                   jax.ShapeDtypeStruct((t, d), BF16),
                   jax.ShapeDtypeStruct((t, d), BF16)],
        compiler_params=_params("arbitrary"),
        name="inproj",
    )(x2d, gain, scale, shift, ws5t, wfn, wg, perm, dft)


def _s5_kernel(x_ref, lre_ref, lim_ref, ls_ref, btr_ref, bti_ref, cr_ref, ci_ref, dcol_ref,
               y_ref, v_scr, w_scr, win_scr, wout_scr, amul_scr, ys_scr, *, chunks_per_seq):
    nc = x_ref.shape[0]
    ns = 2 * S5_STATE
    hg = S5_GROUP
    tiles = CHUNK // SUBLANES
    lane = lax.broadcasted_iota(jnp.int32, (1, ns), 1)
    fwd = lane < S5_STATE

    step = pl.program_id(0)
    cur = step % 2
    prev = 1 - cur

    @pl.when(step == 0)
    def _():
        w_scr[1] = jnp.zeros(w_scr.shape[1:], BF16)
        win_scr[1] = jnp.zeros(win_scr.shape[1:], BF16)
        wout_scr[1] = jnp.zeros(wout_scr.shape[1:], BF16)
        amul_scr[1] = jnp.zeros(amul_scr.shape[1:], F32)

    lam_re = lre_ref[...]
    lam_im = lim_ref[...]
    dt = jnp.exp(ls_ref[...])
    lr = lam_re * dt
    li = lam_im * dt

    def powers(e):
        mag = jnp.exp(e * lr)
        return mag * jnp.cos(e * li), mag * jnp.sin(e * li)

    def cmul(ar, ai, br, bi):
        return ar * br - ai * bi, ar * bi + ai * br

    k_idx = lax.broadcasted_iota(jnp.int32, (3 * SUBLANES, ns), 0)
    base_r, base_i = powers((SUBLANES * k_idx).astype(F32))
    l_idx = lax.broadcasted_iota(jnp.int32, (SUBLANES, ns), 0)

    def base(k_fwd, k_bwd):
        rows = lambda t, k: jnp.broadcast_to(t[k:k + 1], (SUBLANES, ns))
        return (jnp.where(fwd, rows(base_r, k_fwd), rows(base_r, k_bwd)),
                jnp.where(fwd, rows(base_i, k_fwd), rows(base_i, k_bwd)))

    lb_re, lb_im = powers(jnp.ones((1, ns), F32))
    den = lam_re * lam_re + lam_im * lam_im
    num_re = lb_re - 1.0
    num_im = lb_im
    coef_re = (num_re * lam_re + num_im * lam_im) / den
    coef_im = (num_im * lam_re - num_re * lam_im) / den
    btr = btr_ref[...]
    bti = bti_ref[...]
    bbr = coef_re * btr - coef_im * bti
    bbi = coef_re * bti + coef_im * btr
    cr = cr_ref[...]
    ci = ci_ref[...]

    qr = jnp.concatenate([cr[h:h + 1] * bbr - ci[h:h + 1] * bbi for h in range(hg)], axis=0)
    qi = jnp.concatenate([cr[h:h + 1] * bbi + ci[h:h + 1] * bbr for h in range(hg)], axis=0)

    up_r, up_i = powers(l_idx.astype(F32))
    dn_r, dn_i = powers((SUBLANES - l_idx).astype(F32))
    fwd_tile = lax.broadcasted_iota(jnp.int32, (SUBLANES, ns), 1) < S5_STATE
    pr_tiles, pi_tiles = [], []
    for a in range(2 * tiles):
        if a < tiles:
            tr, ti = cmul(*base(tiles - 1 - a, tiles - 1 - a), dn_r, dn_i)
            keep = jnp.logical_not(fwd_tile)
        else:
            tr, ti = cmul(*base(a - tiles, a - tiles), up_r, up_i)
            keep = (fwd_tile | (l_idx == 0)) if a == tiles else fwd_tile
        pr_tiles.append(jnp.where(keep, tr, 0.0))
        pi_tiles.append(jnp.where(keep, ti, 0.0))
    pr = jnp.concatenate(pr_tiles, axis=0)
    pi = jnp.concatenate(pi_tiles, axis=0)
    hi = lax.Precision.HIGHEST
    v = _dot_nt(qr, pr, hi) - _dot_nt(qi, pi, hi)
    centre = lax.broadcasted_iota(jnp.int32, (1, 2 * CHUNK), 1) == CHUNK
    v_scr[...] = v + jnp.where(centre, dcol_ref[...], 0.0)

    def affine_table(k_fwd, k_bwd, ur, ui):
        tr, ti = zip(*[cmul(*base(k_fwd(a), k_bwd(a)), ur, ui) for a in range(tiles)])
        return jnp.concatenate(tr, axis=0), jnp.concatenate(ti, axis=0)

    ur, ui = powers(jnp.where(fwd, SUBLANES - 1 - l_idx, l_idx).astype(F32))
    pin_r, pin_i = affine_table(lambda a: tiles - 1 - a, lambda a: a, ur, ui)
    for hq in range(hg):
        rows = slice(hq * CHUNK, (hq + 1) * CHUNK)
        br = bbr[hq:hq + 1]
        bi = bbi[hq:hq + 1]
        win_scr[cur, rows, :ns] = (pin_r * br - pin_i * bi).astype(BF16)
        win_scr[cur, rows, ns:] = (pin_r * bi + pin_i * br).astype(BF16)

    ur, ui = powers(jnp.where(fwd, l_idx + 1, SUBLANES - l_idx).astype(F32))
    po_r, po_i = affine_table(lambda a: a, lambda a: tiles - 1 - a, ur, ui)
    for h in range(hg):
        rows = slice(h * CHUNK, (h + 1) * CHUNK)
        c_r = cr[h:h + 1]
        c_i = ci[h:h + 1]
        wout_scr[cur, rows, :ns] = (c_r * po_r - c_i * po_i).astype(BF16)
        wout_scr[cur, rows, ns:] = (-(c_r * po_i + c_i * po_r)).astype(BF16)

    ar, ai = base_r[tiles:tiles + 1], base_i[tiles:tiles + 1]
    dist = 1
    while dist < chunks_per_seq:
        row = 2 * (dist.bit_length() - 1)
        amul_scr[cur, row:row + 1] = ar
        amul_scr[cur, row + 1:row + 2] = ai
        ar, ai = cmul(ar, ai, ar, ai)
        dist *= 2

    x = x_ref[...]
    z = _dot(x, win_scr[prev])
    zr = z[:, :ns]
    zi = z[:, ns:]

    pos = lax.broadcasted_iota(jnp.int32, (nc, ns), 0) & (chunks_per_seq - 1)

    def in_seq(dist):
        return ((pos >= jnp.where(fwd, dist, 0))
                & (pos < jnp.where(fwd, chunks_per_seq, chunks_per_seq - dist)))

    def neighbour(a, dist):
        return jnp.where(fwd, pltpu.roll(a, dist, 0), pltpu.roll(a, nc - dist, 0))

    dist = 1
    while dist < chunks_per_seq:
        row = 2 * (dist.bit_length() - 1)
        ar = amul_scr[prev, row:row + 1]
        ai = amul_scr[prev, row + 1:row + 2]
        nr = neighbour(zr, dist)
        ni = neighbour(zi, dist)
        ok = in_seq(dist)
        zr, zi = (zr + jnp.where(ok, ar * nr - ai * ni, 0.0),
                  zi + jnp.where(ok, ar * ni + ai * nr, 0.0))
        dist *= 2
    ok = in_seq(1)
    sr = jnp.where(ok, neighbour(zr, 1), 0.0)
    si = jnp.where(ok, neighbour(zi, 1), 0.0)
    s_in = jnp.concatenate([sr, si], axis=1).astype(BF16)
    ys_scr[...] = _dot_nt(s_in, wout_scr[prev])

    wide = 2 * CHUNK
    for hp in range(hg // 2):
        for hl in range(2):
            for hq in range(hg):
                r = (2 * hp + hl) * hg + hq
                row = jnp.broadcast_to(v_scr[r:r + 1, :], (CHUNK, 2 * CHUNK))
                blk = pltpu.roll(row, CHUNK, 1, stride=1, stride_axis=0)
                w_scr[cur, hp, hq * CHUNK:(hq + 1) * CHUNK, hl * CHUNK:(hl + 1) * CHUNK] = (
                    blk[:, :CHUNK].astype(BF16))
        cols = slice(hp * wide, (hp + 1) * wide)
        y_ref[:, cols] = _dot(x_ref[...], w_scr[prev, hp]) + ys_scr[:, cols]


def _s5(us5, lre, lim, ls, btr, bti, cr, ci, dcol, chunks_per_seq):
    nc, width = us5.shape
    groups = lre.shape[0]
    gw = S5_GROUP * CHUNK
    ns = 2 * S5_STATE
    assert chunks_per_seq & (chunks_per_seq - 1) == 0
    built = lambda s: jnp.minimum(s, groups - 1)
    applied = lambda s: jnp.maximum(s - 1, 0)
    row = pl.BlockSpec((None, 1, ns), lambda s: (built(s), 0, 0))
    mat = pl.BlockSpec((None, S5_GROUP, ns), lambda s: (built(s), 0, 0))
    return pl.pallas_call(
        functools.partial(_s5_kernel, chunks_per_seq=chunks_per_seq),
        grid=(groups + 1,),
        in_specs=[pl.BlockSpec((nc, gw), lambda s: (0, applied(s))),
                  row, row, row, mat, mat, mat, mat,
                  pl.BlockSpec((None, S5_GROUP * S5_GROUP, 1), lambda s: (built(s), 0, 0))],
        out_specs=pl.BlockSpec((nc, gw), lambda s: (0, applied(s))),
        out_shape=jax.ShapeDtypeStruct((nc, width), F32),
        scratch_shapes=[pltpu.VMEM((S5_GROUP * S5_GROUP, 2 * CHUNK), F32),
                        pltpu.VMEM((2, S5_GROUP // 2, gw, 2 * CHUNK), BF16),
                        pltpu.VMEM((2, gw, 2 * ns), BF16),
                        pltpu.VMEM((2, gw, 2 * ns), BF16),
                        pltpu.VMEM((2, 2 * SUBLANES, ns), F32),
                        pltpu.VMEM((nc, gw), F32)],
        compiler_params=_params("arbitrary"),
        name="s5",
    )(us5, lre, lim, ls, btr, bti, cr, ci, dcol)


def _fft_kernel(f_ref, m_ref, v_ref, o_ref, a_scr):
    nq, n2, tile, c = a_scr.shape

    def stage1(j, carry):
        res = _dot(f_ref[...], v_ref[j]).astype(BF16)
        for q in range(nq):
            a_scr[q, j] = res[q * tile:(q + 1) * tile]
        return carry

    lax.fori_loop(0, n2, stage1, 0, unroll=8)

    per = tile // FFT_GROUP
    for k in range(nq // per):
        parts = []
        for qq in range(per):
            q = k * per + qq
            rhs = a_scr[q].reshape(n2 * tile, c)
            parts.append(_dot(m_ref[q], rhs).reshape(n2, FFT_GROUP, c))
        o_ref[:, k * tile:(k + 1) * tile, :] = jnp.concatenate(parts, axis=1).astype(BF16)


def _fft(f1, m2, v):
    batch, slices, n2, rows, cs = v.shape
    tile = 2 * FFT_GROUP
    return pl.pallas_call(
        _fft_kernel,
        grid=(batch, slices),
        in_specs=[_const_spec(f1.shape), _const_spec(m2.shape),
                  pl.BlockSpec((None, None, n2, rows, cs), lambda b, j: (b, j, 0, 0, 0))],
        out_specs=pl.BlockSpec((None, n2, rows // 2, cs), lambda b, j: (b, 0, 0, j)),
        out_shape=jax.ShapeDtypeStruct((batch, n2, rows // 2, slices * cs), BF16),
        scratch_shapes=[pltpu.VMEM((rows // tile, n2, tile, cs), BF16)],
        compiler_params=_params("arbitrary", "arbitrary"),
        name="fft",
    )(f1, m2, v)


@functools.lru_cache(maxsize=None)
def _chan_dft_table():
    n = FNET_GROUP
    idx = np.arange(n)
    ang = 2.0 * np.pi * ((idx[:, None] * idx[None, :]) % n) / n
    return (np.concatenate([np.cos(ang), -np.sin(ang)], axis=1) / math.sqrt(n)).astype(np.float32)


@functools.lru_cache(maxsize=None)
def _perm_table(tm):
    n2 = tm // FFT_GROUP
    dst = np.arange(tm)
    p = np.zeros((tm, tm), np.float32)
    p[dst, (dst % FFT_GROUP) * n2 + dst // FFT_GROUP] = 1.0
    return p.astype(BF16)


@functools.lru_cache(maxsize=None)
def _fft1_table(n1, seq):
    g = FFT_GROUP
    idx = np.arange(n1)
    ang = 2.0 * np.pi * ((idx[:, None] * idx[None, :]) % n1) / n1
    c, s = np.cos(ang), np.sin(ang)
    blocks = np.array([[c, s], [-s, c]])
    t = blocks.reshape(2, 2, n1 // g, g, n1 // g, g).transpose(2, 0, 3, 4, 1, 5)
    return (t.reshape(2 * n1, 2 * n1) / math.sqrt(seq)).astype(np.float32)


@functools.lru_cache(maxsize=None)
def _fft2_table(n1, n2):
    g = FFT_GROUP
    n = n1 * n2
    q = np.arange(n1 // g)[:, None, None, None]
    k2 = np.arange(n2)[None, :, None, None]
    k1l = np.arange(g)[None, None, :, None]
    nn = np.arange(n2)[None, None, None, :]
    ang = 2.0 * np.pi * ((nn * (g * q + k1l + n1 * k2)) % n) / n
    cs = np.stack([np.cos(ang), np.sin(ang)], axis=-1).astype(np.float32)
    m = np.zeros((n1 // g, n2, g, n2, 2, g), np.float32)
    for l in range(g):
        m[:, :, l, :, :, l] = cs[:, :, l]
    return m.reshape(n1 // g, n2 * g, n2 * 2 * g)


def _merge_kernel(x_ref, ys5_ref, yfn_ref, gs_ref, gf_ref, gm_ref, g2_ref, sc_ref, sh_ref,
                  wglu_ref, wbs_ref, wbf_ref, wo_ref, x1_ref, h2_ref, m_scr):
    tm, d = x_ref.shape
    w5 = wbs_ref.shape[0]
    parts = [jax.nn.gelu(ys5_ref[ci].T, approximate=True).astype(BF16)
             for ci in range(tm // CHUNK)]
    ge = jnp.concatenate(parts, axis=0)
    z = _dot(ge, wglu_ref[...])
    ys = (z[:, :w5] * jax.nn.sigmoid(z[:, w5:])).astype(BF16)
    yfn = yfn_ref[...]
    nb = 512
    for n0 in range(0, d, nb):
        cols = slice(n0, n0 + nb)
        m_scr[:, cols] = (gs_ref[:, cols] * _dot(ys, wbs_ref[:, cols]).astype(BF16)
                          + gf_ref[:, cols] * _dot(yfn, wbf_ref[:, cols]).astype(BF16))
    rg = 256
    for r0 in range(0, tm, rg):
        rows = slice(r0, r0 + rg)
        x1 = x_ref[rows, :] + gm_ref[...] * _dot(m_scr[rows, :], wo_ref[...])
        x1_ref[rows, :] = x1
        h2_ref[rows, :] = _rms_modulate(x1, g2_ref[...], sc_ref[...], sh_ref[...]).astype(BF16)


def _merge(x2d, ys5, yfn, gs, gf, gm, g2, scale, shift, wglu, wbs, wbf, wo, seq):
    t, d = x2d.shape
    tm = ROW_TILE
    bps = seq // tm
    w5 = ys5.shape[1]
    wf = yfn.shape[1]
    rows = lambda width: pl.BlockSpec((tm, width), lambda i: (i, 0))
    per_batch = pl.BlockSpec((None, 1, d), lambda i: (i // bps, 0, 0))
    return pl.pallas_call(
        _merge_kernel,
        grid=(t // tm,),
        in_specs=[rows(d),
                  pl.BlockSpec((tm // CHUNK, w5, CHUNK), lambda i: (i, 0, 0)),
                  rows(wf), rows(d), rows(d),
                  per_batch, _const_spec((1, d)), per_batch, per_batch,
                  _const_spec(wglu.shape), _const_spec(wbs.shape), _const_spec(wbf.shape),
                  _const_spec(wo.shape)],
        out_specs=[rows(d), rows(d)],
        out_shape=[jax.ShapeDtypeStruct((t, d), F32), jax.ShapeDtypeStruct((t, d), BF16)],
        scratch_shapes=[pltpu.VMEM((tm, d), BF16)],
        compiler_params=_params("arbitrary"),
        name="merge",
    )(x2d, ys5, yfn, gs, gf, gm, g2, scale, shift, wglu, wbs, wbf, wo)


def _ffn_kernel(h_ref, wa_ref, wb_ref, wo_ref, x1_hbm, gate_ref, gn_ref, o_ref, x1_buf, x1_sem):
    i = pl.program_id(0)
    f = pl.program_id(1)
    tm, d = o_ref.shape

    def x1_copy():
        return pltpu.make_async_copy(x1_hbm.at[pl.ds(i * tm, tm), :], x1_buf, x1_sem)

    def hidden(rows):
        h = h_ref[rows, :]
        a = _dot(h, wa_ref[...])
        b = _dot(h, wb_ref[...])
        return (a * jax.nn.sigmoid(a) * b).astype(BF16)

    nb = 512
    last = pl.num_programs(1) - 1

    @pl.when(f == 0)
    def _():
        x1_copy().start()
        act = hidden(slice(None))
        for n0 in range(0, d, nb):
            o_ref[:, n0:n0 + nb] = _dot(act, wo_ref[:, n0:n0 + nb])

    @pl.when((f > 0) & (f < last))
    def _():
        act = hidden(slice(None))
        for n0 in range(0, d, nb):
            o_ref[:, n0:n0 + nb] += _dot(act, wo_ref[:, n0:n0 + nb])

    @pl.when(f == last)
    def _():
        x1_copy().wait()
        rg = 256
        for r0 in range(0, tm, rg):
            rows = slice(r0, r0 + rg)
            acc = o_ref[rows, :] + _dot(hidden(rows), wo_ref[...])
            x2 = x1_buf[rows, :] + gate_ref[...] * acc
            ms = jnp.mean(x2 * x2, axis=-1, keepdims=True)
            o_ref[rows, :] = (x2 * lax.rsqrt(ms + EPS)) * gn_ref[...]


def _ffn(h2, w_in, w_out, x1, gate, gn, seq):
    t, d = h2.shape
    ff = w_out.shape[0]
    tm, tf = FFN_ROW_TILE, FFN_TILE
    bps = seq // tm
    nf = ff // tf
    assert nf >= 2
    w_in = w_in.reshape(d, 2 * nf, tf).transpose(1, 0, 2).astype(BF16)
    return pl.pallas_call(
        _ffn_kernel,
        grid=(t // tm, nf),
        in_specs=[pl.BlockSpec((tm, d), lambda i, f: (i, 0)),
                  pl.BlockSpec((None, d, tf), lambda i, f: (f, 0, 0)),
                  pl.BlockSpec((None, d, tf), lambda i, f: (f + nf, 0, 0)),
                  pl.BlockSpec((tf, d), lambda i, f: (f, 0)),
                  pl.BlockSpec(memory_space=pl.ANY),
                  pl.BlockSpec((None, 1, d), lambda i, f: (i // bps, 0, 0)),
                  pl.BlockSpec((1, d), lambda i, f: (0, 0))],
        out_specs=pl.BlockSpec((tm, d), lambda i, f: (i, 0)),
        out_shape=jax.ShapeDtypeStruct((t, d), F32),
        scratch_shapes=[pltpu.VMEM((tm, d), F32), pltpu.SemaphoreType.DMA],
        compiler_params=_params("arbitrary", "arbitrary"),
        name="ffn",
    )(h2, w_in, w_in, w_out, x1, gate, gn)


def _s5_param_layout(lam_re, lam_im, log_step, b_re, b_im, c_re, c_im, d_skip):
    groups = lam_re.shape[1]
    both = lambda p: jnp.concatenate([p[0], p[1]], axis=-1)
    row = lambda p: both(p)[:, None, :]
    ls = jnp.broadcast_to(log_step[:, :, None], log_step.shape + (S5_STATE,))
    bt = lambda p: both(jnp.swapaxes(p, -1, -2))
    eye = jnp.eye(S5_GROUP, dtype=F32)
    dcol = (d_skip.reshape(groups, S5_GROUP, 1) * eye[None]).reshape(groups, S5_GROUP * S5_GROUP, 1)
    return (row(lam_re), row(lam_im), row(ls), bt(b_re), bt(b_im), both(c_re), both(c_im), dcol)


def kernel(x, c, w_ada, b_ada, norm_mix, w_in, s5_lambda_re, s5_lambda_im, s5_log_step,
           s5_b_re, s5_b_im, s5_c_re, s5_c_im, s5_d, w_s5_glu, w_branch_s5, w_branch_fnet,
           w_out, norm_ffn, w_ffn_in, w_ffn_out, norm_final):
    batch, seq, d = x.shape
    t = batch * seq
    w5 = s5_d.shape[-1]
    wf = w_branch_fnet.shape[1]
    n2 = FFT_N2
    n1 = seq // n2
    assert w_in.shape[0] == 1, "multi-layer stacks are not supported"
    assert seq % FFN_ROW_TILE == 0 and n1 % BF16_ROWS == 0

    dft = jnp.asarray(_chan_dft_table()).astype(BF16)
    perm = jnp.asarray(_perm_table(ROW_TILE))
    f1 = jnp.asarray(_fft1_table(n1, seq)).astype(BF16)
    m2 = jnp.asarray(_fft2_table(n1, n2)).astype(BF16)

    xs = x.reshape(t, d).astype(F32)
    ct = c.astype(F32).T
    for l in range(w_in.shape[0]):
        mod = _ada(ct, w_ada[l], b_ada[l])
        sh_m, sc_m, g_m, sh_f, sc_f, g_f = [m.reshape(batch, 1, d)
                                            for m in jnp.split(mod, N_MOD, axis=-1)]
        wl = w_in[l]
        us5, v, gs, gf = _inproj(
            xs, norm_mix[l].reshape(1, d), sc_m, sh_m,
            wl[:, :w5].T.astype(BF16), wl[:, w5:w5 + wf].astype(BF16),
            wl[:, w5 + wf:].astype(BF16), perm, dft, batch, seq)

        s5p = _s5_param_layout(s5_lambda_re[l], s5_lambda_im[l], s5_log_step[l], s5_b_re[l],
                               s5_b_im[l], s5_c_re[l], s5_c_im[l], s5_d[l])
        ys5 = _s5(us5.reshape(t // CHUNK, w5 * CHUNK), *s5p, chunks_per_seq=seq // CHUNK)

        yfn = _fft(f1, m2, v)

        xs, h2 = _merge(
            xs, ys5.reshape(t // CHUNK, w5, CHUNK), yfn.reshape(t, wf), gs, gf, g_m,
            norm_ffn[l].reshape(1, d), sc_f, sh_f, w_s5_glu[l].astype(BF16),
            w_branch_s5[l].astype(BF16), w_branch_fnet[l].astype(BF16), w_out[l].astype(BF16), seq)
        xs = _ffn(h2, w_ffn_in[l], w_ffn_out[l].astype(BF16), xs, g_f,
                  norm_final.reshape(1, d), seq)
    return xs.reshape(batch, seq, d).astype(x.dtype)
```

```python
import functools
import math

import jax
import jax.numpy as jnp
import numpy as np
from jax import lax
from jax.experimental import pallas as pl
from jax.experimental.pallas import tpu as pltpu

F32 = jnp.float32
BF16 = jnp.bfloat16

S5_GROUP = 16
S5_STATE = 64
FNET_GROUP = 256
N_MOD = 6
EPS = 1e-6

LANES = 128
SUBLANES = 8
BF16_ROWS = 16
VMEM_LIMIT_BYTES = 60 * 1024 * 1024

CHUNK = LANES

ROW_TILE = 512
FFN_ROW_TILE = 1024
FFN_TILE = 512
ADA_TILE = 1024

FFT_GROUP = SUBLANES
FFT_N2 = ROW_TILE // FFT_GROUP


def _params(*semantics):
    return pltpu.CompilerParams(dimension_semantics=semantics, vmem_limit_bytes=VMEM_LIMIT_BYTES)


def _const_spec(shape):
    zeros = (0,) * len(shape)
    return pl.BlockSpec(shape, lambda *_: zeros, pipeline_mode=pl.Buffered(1))


def _dot(a, b):
    return jnp.dot(a, b, preferred_element_type=F32)


def _dot_nt(a, b, precision=None):
    return lax.dot_general(a, b, (((1,), (1,)), ((), ())), precision=precision,
                           preferred_element_type=F32)


def _rms_modulate(x, gain, scale, shift):
    ms = jnp.mean(x * x, axis=-1, keepdims=True)
    y = x * lax.rsqrt(ms + EPS)
    return (y * gain) * (1.0 + scale) + shift


def _ada_kernel(ct_ref, w_ref, b_ref, o_ref, cb_scr):
    d, nb = ct_ref.shape
    tn = o_ref.shape[1]

    @pl.when(pl.program_id(0) == 0)
    def _():
        cs = ct_ref[...]
        cs = cs * jax.nn.sigmoid(cs)
        for b in range(nb):
            cb_scr[b] = jnp.broadcast_to(cs[:, b:b + 1], (d, LANES))

    def body(k, accs):
        k0 = pl.multiple_of(k * SUBLANES, SUBLANES)
        w = w_ref[pl.ds(k0, SUBLANES), :]
        return tuple(acc + w * jnp.tile(cb_scr[b, pl.ds(k0, SUBLANES), :], (1, tn // LANES))
                     for b, acc in enumerate(accs))

    init = tuple(jnp.zeros((SUBLANES, tn), F32) for _ in range(nb))
    accs = lax.fori_loop(0, d // SUBLANES, body, init, unroll=8)
    for b, acc in enumerate(accs):
        o_ref[b:b + 1, :] = jnp.sum(acc, axis=0, keepdims=True) + b_ref[...]


def _ada(ct, w, bias):
    d, nb = ct.shape
    n = w.shape[1]
    return pl.pallas_call(
        _ada_kernel,
        grid=(n // ADA_TILE,),
        in_specs=[pl.BlockSpec((d, nb), lambda j: (0, 0)),
                  pl.BlockSpec((d, ADA_TILE), lambda j: (0, j)),
                  pl.BlockSpec((1, ADA_TILE), lambda j: (0, j))],
        out_specs=pl.BlockSpec((nb, ADA_TILE), lambda j: (0, j)),
        out_shape=jax.ShapeDtypeStruct((nb, n), F32),
        scratch_shapes=[pltpu.VMEM((nb, d, LANES), F32)],
        compiler_params=_params("arbitrary"),
        name="ada",
    )(ct, w, bias.reshape(1, n))


def _inproj_kernel(x_ref, g_ref, sc_ref, sh_ref, ws5t_ref, wfn_ref, wg_ref, perm_ref, dft_ref,
                   us5_ref, v_ref, gs_ref, gf_ref):
    tm, d = x_ref.shape
    hb = _rms_modulate(x_ref[...], g_ref[...], sc_ref[...], sh_ref[...]).astype(BF16)

    ut = _dot_nt(ws5t_ref[...], hb)
    for ci in range(tm // CHUNK):
        us5_ref[ci] = ut[:, ci * CHUNK:(ci + 1) * CHUNK].astype(BF16)

    uf = _dot(hb, wfn_ref[...]).astype(BF16)
    ufp = _dot(perm_ref[...], uf).astype(BF16)
    n2 = tm // FFT_GROUP
    for g in range(uf.shape[1] // FNET_GROUP):
        cols = slice(g * FNET_GROUP, (g + 1) * FNET_GROUP)
        pq = _dot(ufp[:, cols], dft_ref[...])
        re = pq[:, :FNET_GROUP].reshape(n2, FFT_GROUP, FNET_GROUP)
        im = pq[:, FNET_GROUP:].reshape(n2, FFT_GROUP, FNET_GROUP)
        v_ref[:, :, cols] = jnp.concatenate([re, im], axis=1).astype(BF16)

    nb = 512
    for n0 in range(0, 2 * d, nb):
        s = jax.nn.sigmoid(_dot(hb, wg_ref[:, n0:n0 + nb])).astype(BF16)
        if n0 < d:
            gs_ref[:, n0:n0 + nb] = s
        else:
            gf_ref[:, n0 - d:n0 - d + nb] = s


def _inproj(x2d, gain, scale, shift, ws5t, wfn, wg, perm, dft, batch, seq):
    t, d = x2d.shape
    tm = ROW_TILE
    bps = seq // tm
    w5 = ws5t.shape[0]
    wf = wfn.shape[1]
    vrows = 2 * FFT_GROUP
    per_batch = pl.BlockSpec((None, 1, d), lambda i: (i // bps, 0, 0))
    return pl.pallas_call(
        _inproj_kernel,
        grid=(t // tm,),
        in_specs=[pl.BlockSpec((tm, d), lambda i: (i, 0)),
                  _const_spec((1, d)), per_batch, per_batch,
                  _const_spec(ws5t.shape), _const_spec(wfn.shape), _const_spec(wg.shape),
                  _const_spec(perm.shape), _const_spec(dft.shape)],
        out_specs=[pl.BlockSpec((tm // CHUNK, w5, CHUNK), lambda i: (i, 0, 0)),
                   pl.BlockSpec((None, FFT_N2, vrows, wf), lambda i: (i // bps, 0, i % bps, 0)),
                   pl.BlockSpec((tm, d), lambda i: (i, 0)),
                   pl.BlockSpec((tm, d), lambda i: (i, 0))],
        out_shape=[jax.ShapeDtypeStruct((t // CHUNK, w5, CHUNK), BF16),
                   jax.ShapeDtypeStruct((batch, FFT_N2, bps * vrows, wf), BF16),
                   jax.ShapeDtypeStruct((t, d), BF16),
                   jax.ShapeDtypeStruct((t, d), BF16)],
        compiler_params=_params("arbitrary"),
        name="inproj",
    )(x2d, gain, scale, shift, ws5t, wfn, wg, perm, dft)


def _s5_kernel(*refs, chunks_per_seq):
    w_scr, win_scr, wout_scr, amul_scr = refs[11:15]
    step = pl.program_id(0)

    @pl.when(step == 0)
    def _():
        w_scr[1] = jnp.zeros(w_scr.shape[1:], BF16)
        win_scr[1] = jnp.zeros(win_scr.shape[1:], BF16)
        wout_scr[1] = jnp.zeros(wout_scr.shape[1:], BF16)
        amul_scr[1] = jnp.zeros(amul_scr.shape[1:], F32)

    for cur in range(2):
        @pl.when(step % 2 == cur)
        def _():
            _s5_step(*refs, cur=cur, prev=1 - cur, chunks_per_seq=chunks_per_seq)


def _s5_step(x_ref, lre_ref, lim_ref, ls_ref, btr_ref, bti_ref, cr_ref, ci_ref, dcol_ref,
             y_ref, v_scr, w_scr, win_scr, wout_scr, amul_scr, ys_scr, *, cur, prev, chunks_per_seq):
    nc = x_ref.shape[0]
    ns = 2 * S5_STATE
    hg = S5_GROUP
    tiles = CHUNK // SUBLANES
    lane = lax.broadcasted_iota(jnp.int32, (1, ns), 1)
    fwd = lane < S5_STATE

    lam_re = lre_ref[...]
    lam_im = lim_ref[...]
    dt = jnp.exp(ls_ref[...])
    lr = lam_re * dt
    li = lam_im * dt

    def powers(e):
        mag = jnp.exp(e * lr)
        return mag * jnp.cos(e * li), mag * jnp.sin(e * li)

    def cmul(ar, ai, br, bi):
        return ar * br - ai * bi, ar * bi + ai * br

    k_idx = lax.broadcasted_iota(jnp.int32, (3 * SUBLANES, ns), 0)
    base_r, base_i = powers((SUBLANES * k_idx).astype(F32))
    l_idx = lax.broadcasted_iota(jnp.int32, (SUBLANES, ns), 0)

    def base(k_fwd, k_bwd):
        rows = lambda t, k: jnp.broadcast_to(t[k:k + 1], (SUBLANES, ns))
        return (jnp.where(fwd, rows(base_r, k_fwd), rows(base_r, k_bwd)),
                jnp.where(fwd, rows(base_i, k_fwd), rows(base_i, k_bwd)))

    lb_re, lb_im = powers(jnp.ones((1, ns), F32))
    den = lam_re * lam_re + lam_im * lam_im
    num_re = lb_re - 1.0
    num_im = lb_im
    coef_re = (num_re * lam_re + num_im * lam_im) / den
    coef_im = (num_im * lam_re - num_re * lam_im) / den
    btr = btr_ref[...]
    bti = bti_ref[...]
    bbr = coef_re * btr - coef_im * bti
    bbi = coef_re * bti + coef_im * btr
    cr = cr_ref[...]
    ci = ci_ref[...]

    qr = jnp.concatenate([cr[h:h + 1] * bbr - ci[h:h + 1] * bbi for h in range(hg)], axis=0)
    qi = jnp.concatenate([cr[h:h + 1] * bbi + ci[h:h + 1] * bbr for h in range(hg)], axis=0)

    up_r, up_i = powers(l_idx.astype(F32))
    dn_r, dn_i = powers((SUBLANES - l_idx).astype(F32))
    fwd_tile = lax.broadcasted_iota(jnp.int32, (SUBLANES, ns), 1) < S5_STATE
    pr_tiles, pi_tiles = [], []
    for a in range(2 * tiles):
        if a < tiles:
            tr, ti = cmul(*base(tiles - 1 - a, tiles - 1 - a), dn_r, dn_i)
            keep = jnp.logical_not(fwd_tile)
        else:
            tr, ti = cmul(*base(a - tiles, a - tiles), up_r, up_i)
            keep = (fwd_tile | (l_idx == 0)) if a == tiles else fwd_tile
        pr_tiles.append(jnp.where(keep, tr, 0.0))
        pi_tiles.append(jnp.where(keep, ti, 0.0))
    pr = jnp.concatenate(pr_tiles, axis=0)
    pi = jnp.concatenate(pi_tiles, axis=0)
    hi = lax.Precision.HIGHEST
    v = _dot_nt(qr, pr, hi) - _dot_nt(qi, pi, hi)
    centre = lax.broadcasted_iota(jnp.int32, (1, 2 * CHUNK), 1) == CHUNK
    v_scr[...] = v + jnp.where(centre, dcol_ref[...], 0.0)

    def affine_table(k_fwd, k_bwd, ur, ui):
        tr, ti = zip(*[cmul(*base(k_fwd(a), k_bwd(a)), ur, ui) for a in range(tiles)])
        return jnp.concatenate(tr, axis=0), jnp.concatenate(ti, axis=0)

    ur, ui = powers(jnp.where(fwd, SUBLANES - 1 - l_idx, l_idx).astype(F32))
    pin_r, pin_i = affine_table(lambda a: tiles - 1 - a, lambda a: a, ur, ui)
    for hq in range(hg):
        rows = slice(hq * CHUNK, (hq + 1) * CHUNK)
        br = bbr[hq:hq + 1]
        bi = bbi[hq:hq + 1]
        win_scr[cur, rows, :ns] = (pin_r * br - pin_i * bi).astype(BF16)
        win_scr[cur, rows, ns:] = (pin_r * bi + pin_i * br).astype(BF16)

    ur, ui = powers(jnp.where(fwd, l_idx + 1, SUBLANES - l_idx).astype(F32))
    po_r, po_i = affine_table(lambda a: a, lambda a: tiles - 1 - a, ur, ui)
    for h in range(hg):
        rows = slice(h * CHUNK, (h + 1) * CHUNK)
        c_r = cr[h:h + 1]
        c_i = ci[h:h + 1]
        wout_scr[cur, rows, :ns] = (c_r * po_r - c_i * po_i).astype(BF16)
        wout_scr[cur, rows, ns:] = (-(c_r * po_i + c_i * po_r)).astype(BF16)

    ar, ai = base_r[tiles:tiles + 1], base_i[tiles:tiles + 1]
    dist = 1
    while dist < chunks_per_seq:
        row = 2 * (dist.bit_length() - 1)
        amul_scr[cur, row:row + 1] = ar
        amul_scr[cur, row + 1:row + 2] = ai
        ar, ai = cmul(ar, ai, ar, ai)
        dist *= 2

    x = x_ref[...]
    z = _dot(x, win_scr[prev])
    zr = z[:, :ns]
    zi = z[:, ns:]

    pos = lax.broadcasted_iota(jnp.int32, (nc, ns), 0) & (chunks_per_seq - 1)

    def in_seq(dist):
        return ((pos >= jnp.where(fwd, dist, 0))
                & (pos < jnp.where(fwd, chunks_per_seq, chunks_per_seq - dist)))

    def neighbour(a, dist):
        return jnp.where(fwd, pltpu.roll(a, dist, 0), pltpu.roll(a, nc - dist, 0))

    dist = 1
    while dist < chunks_per_seq:
        row = 2 * (dist.bit_length() - 1)
        ar = amul_scr[prev, row:row + 1]
        ai = amul_scr[prev, row + 1:row + 2]
        nr = neighbour(zr, dist)
        ni = neighbour(zi, dist)
        ok = in_seq(dist)
        zr, zi = (zr + jnp.where(ok, ar * nr - ai * ni, 0.0),
                  zi + jnp.where(ok, ar * ni + ai * nr, 0.0))
        dist *= 2
    ok = in_seq(1)
    sr = jnp.where(ok, neighbour(zr, 1), 0.0)
    si = jnp.where(ok, neighbour(zi, 1), 0.0)
    s_in = jnp.concatenate([sr, si], axis=1).astype(BF16)
    ys_scr[...] = _dot_nt(s_in, wout_scr[prev])

    wide = 2 * CHUNK
    for hp in range(hg // 2):
        for hl in range(2):
            for hq in range(hg):
                r = (2 * hp + hl) * hg + hq
                row = jnp.broadcast_to(v_scr[r:r + 1, :], (CHUNK, 2 * CHUNK))
                blk = pltpu.roll(row, CHUNK, 1, stride=1, stride_axis=0)
                w_scr[cur, hp, hq * CHUNK:(hq + 1) * CHUNK, hl * CHUNK:(hl + 1) * CHUNK] = (
                    blk[:, :CHUNK].astype(BF16))
        cols = slice(hp * wide, (hp + 1) * wide)
        y_ref[:, cols] = _dot(x_ref[...], w_scr[prev, hp]) + ys_scr[:, cols]


def _s5(us5, lre, lim, ls, btr, bti, cr, ci, dcol, chunks_per_seq):
    nc, width = us5.shape
    groups = lre.shape[0]
    gw = S5_GROUP * CHUNK
    ns = 2 * S5_STATE
    assert chunks_per_seq & (chunks_per_seq - 1) == 0
    built = lambda s: jnp.minimum(s, groups - 1)
    applied = lambda s: jnp.maximum(s - 1, 0)
    row = pl.BlockSpec((None, 1, ns), lambda s: (built(s), 0, 0))
    mat = pl.BlockSpec((None, S5_GROUP, ns), lambda s: (built(s), 0, 0))
    return pl.pallas_call(
        functools.partial(_s5_kernel, chunks_per_seq=chunks_per_seq),
        grid=(groups + 1,),
        in_specs=[pl.BlockSpec((nc, gw), lambda s: (0, applied(s))),
                  row, row, row, mat, mat, mat, mat,
                  pl.BlockSpec((None, S5_GROUP * S5_GROUP, 1), lambda s: (built(s), 0, 0))],
        out_specs=pl.BlockSpec((nc, gw), lambda s: (0, applied(s))),
        out_shape=jax.ShapeDtypeStruct((nc, width), F32),
        scratch_shapes=[pltpu.VMEM((S5_GROUP * S5_GROUP, 2 * CHUNK), F32),
                        pltpu.VMEM((2, S5_GROUP // 2, gw, 2 * CHUNK), BF16),
                        pltpu.VMEM((2, gw, 2 * ns), BF16),
                        pltpu.VMEM((2, gw, 2 * ns), BF16),
                        pltpu.VMEM((2, 2 * SUBLANES, ns), F32),
                        pltpu.VMEM((nc, gw), F32)],
        compiler_params=_params("arbitrary"),
        name="s5",
    )(us5, lre, lim, ls, btr, bti, cr, ci, dcol)


def _fft_kernel(f_ref, m_ref, v_ref, o_ref, a_scr):
    nq, n2, tile, c = a_scr.shape

    def stage1(j, carry):
        res = _dot(f_ref[...], v_ref[j]).astype(BF16)
        for q in range(nq):
            a_scr[q, j] = res[q * tile:(q + 1) * tile]
        return carry

    lax.fori_loop(0, n2, stage1, 0, unroll=8)

    per = tile // FFT_GROUP
    for k in range(nq // per):
        parts = []
        for qq in range(per):
            q = k * per + qq
            rhs = a_scr[q].reshape(n2 * tile, c)
            parts.append(_dot(m_ref[q], rhs).reshape(n2, FFT_GROUP, c))
        o_ref[:, k * tile:(k + 1) * tile, :] = jnp.concatenate(parts, axis=1).astype(BF16)


def _fft(f1, m2, v):
    batch, n2, rows, c = v.shape
    tile = 2 * FFT_GROUP
    cs = FNET_GROUP
    return pl.pallas_call(
        _fft_kernel,
        grid=(batch, c // cs),
        in_specs=[_const_spec(f1.shape), _const_spec(m2.shape),
                  pl.BlockSpec((None, n2, rows, cs), lambda b, j: (b, 0, 0, j))],
        out_specs=pl.BlockSpec((None, n2, rows // 2, cs), lambda b, j: (b, 0, 0, j)),
        out_shape=jax.ShapeDtypeStruct((batch, n2, rows // 2, c), BF16),
        scratch_shapes=[pltpu.VMEM((rows // tile, n2, tile, cs), BF16)],
        compiler_params=_params("arbitrary", "arbitrary"),
        name="fft",
    )(f1, m2, v)


@functools.lru_cache(maxsize=None)
def _chan_dft_table():
    n = FNET_GROUP
    idx = np.arange(n)
    ang = 2.0 * np.pi * ((idx[:, None] * idx[None, :]) % n) / n
    return (np.concatenate([np.cos(ang), -np.sin(ang)], axis=1) / math.sqrt(n)).astype(np.float32)


@functools.lru_cache(maxsize=None)
def _perm_table(tm):
    n2 = tm // FFT_GROUP
    dst = np.arange(tm)
    p = np.zeros((tm, tm), np.float32)
    p[dst, (dst % FFT_GROUP) * n2 + dst // FFT_GROUP] = 1.0
    return p.astype(BF16)


@functools.lru_cache(maxsize=None)
def _fft1_table(n1, seq):
    g = FFT_GROUP
    idx = np.arange(n1)
    ang = 2.0 * np.pi * ((idx[:, None] * idx[None, :]) % n1) / n1
    c, s = np.cos(ang), np.sin(ang)
    blocks = np.array([[c, s], [-s, c]])
    t = blocks.reshape(2, 2, n1 // g, g, n1 // g, g).transpose(2, 0, 3, 4, 1, 5)
    return (t.reshape(2 * n1, 2 * n1) / math.sqrt(seq)).astype(np.float32)


@functools.lru_cache(maxsize=None)
def _fft2_table(n1, n2):
    g = FFT_GROUP
    n = n1 * n2
    q = np.arange(n1 // g)[:, None, None, None]
    k2 = np.arange(n2)[None, :, None, None]
    k1l = np.arange(g)[None, None, :, None]
    nn = np.arange(n2)[None, None, None, :]
    ang = 2.0 * np.pi * ((nn * (g * q + k1l + n1 * k2)) % n) / n
    cs = np.stack([np.cos(ang), np.sin(ang)], axis=-1).astype(np.float32)
    m = np.zeros((n1 // g, n2, g, n2, 2, g), np.float32)
    for l in range(g):
        m[:, :, l, :, :, l] = cs[:, :, l]
    return m.reshape(n1 // g, n2 * g, n2 * 2 * g)


def _merge_kernel(x_ref, ys5_ref, yfn_ref, gs_ref, gf_ref, gm_ref, g2_ref, sc_ref, sh_ref,
                  wglu_ref, wbs_ref, wbf_ref, wo_ref, x1_ref, h2_ref, m_scr):
    tm, d = x_ref.shape
    w5 = wbs_ref.shape[0]
    parts = [jax.nn.gelu(ys5_ref[ci].T, approximate=True).astype(BF16)
             for ci in range(tm // CHUNK)]
    ge = jnp.concatenate(parts, axis=0)
    z = _dot(ge, wglu_ref[...])
    ys = (z[:, :w5] * jax.nn.sigmoid(z[:, w5:])).astype(BF16)
    yfn = yfn_ref[...]
    nb = 512
    for n0 in range(0, d, nb):
        cols = slice(n0, n0 + nb)
        m_scr[:, cols] = (gs_ref[:, cols] * _dot(ys, wbs_ref[:, cols]).astype(BF16)
                          + gf_ref[:, cols] * _dot(yfn, wbf_ref[:, cols]).astype(BF16))
    rg = 256
    for r0 in range(0, tm, rg):
        rows = slice(r0, r0 + rg)
        x1 = x_ref[rows, :] + gm_ref[...] * _dot(m_scr[rows, :], wo_ref[...])
        x1_ref[rows, :] = x1
        h2_ref[rows, :] = _rms_modulate(x1, g2_ref[...], sc_ref[...], sh_ref[...]).astype(BF16)


def _merge(x2d, ys5, yfn, gs, gf, gm, g2, scale, shift, wglu, wbs, wbf, wo, seq):
    t, d = x2d.shape
    tm = ROW_TILE
    bps = seq // tm
    w5 = ys5.shape[1]
    wf = yfn.shape[1]
    rows = lambda width: pl.BlockSpec((tm, width), lambda i: (i, 0))
    per_batch = pl.BlockSpec((None, 1, d), lambda i: (i // bps, 0, 0))
    return pl.pallas_call(
        _merge_kernel,
        grid=(t // tm,),
        in_specs=[rows(d),
                  pl.BlockSpec((tm // CHUNK, w5, CHUNK), lambda i: (i, 0, 0)),
                  rows(wf), rows(d), rows(d),
                  per_batch, _const_spec((1, d)), per_batch, per_batch,
                  _const_spec(wglu.shape), _const_spec(wbs.shape), _const_spec(wbf.shape),
                  _const_spec(wo.shape)],
        out_specs=[rows(d), rows(d)],
        out_shape=[jax.ShapeDtypeStruct((t, d), F32), jax.ShapeDtypeStruct((t, d), BF16)],
        scratch_shapes=[pltpu.VMEM((tm, d), BF16)],
        compiler_params=_params("arbitrary"),
        name="merge",
    )(x2d, ys5, yfn, gs, gf, gm, g2, scale, shift, wglu, wbs, wbf, wo)


def _ffn_kernel(h_ref, wa_ref, wb_ref, wo_ref, x1_hbm, gate_ref, gn_ref, o_ref, x1_buf, x1_sem):
    i = pl.program_id(0)
    f = pl.program_id(1)
    tm, d = o_ref.shape

    def x1_copy():
        return pltpu.make_async_copy(x1_hbm.at[pl.ds(i * tm, tm), :], x1_buf, x1_sem)

    def hidden(rows):
        h = h_ref[rows, :]
        a = _dot(h, wa_ref[...])
        b = _dot(h, wb_ref[...])
        return (a * jax.nn.sigmoid(a) * b).astype(BF16)

    nb = 512
    last = pl.num_programs(1) - 1

    @pl.when(f == 0)
    def _():
        x1_copy().start()
        act = hidden(slice(None))
        for n0 in range(0, d, nb):
            o_ref[:, n0:n0 + nb] = _dot(act, wo_ref[:, n0:n0 + nb])

    @pl.when((f > 0) & (f < last))
    def _():
        act = hidden(slice(None))
        for n0 in range(0, d, nb):
            o_ref[:, n0:n0 + nb] += _dot(act, wo_ref[:, n0:n0 + nb])

    @pl.when(f == last)
    def _():
        x1_copy().wait()
        rg = 256
        for r0 in range(0, tm, rg):
            rows = slice(r0, r0 + rg)
            acc = o_ref[rows, :] + _dot(hidden(rows), wo_ref[...])
            x2 = x1_buf[rows, :] + gate_ref[...] * acc
            ms = jnp.mean(x2 * x2, axis=-1, keepdims=True)
            o_ref[rows, :] = (x2 * lax.rsqrt(ms + EPS)) * gn_ref[...]


def _ffn(h2, w_in, w_out, x1, gate, gn, seq):
    t, d = h2.shape
    ff = w_out.shape[0]
    tm, tf = FFN_ROW_TILE, FFN_TILE
    bps = seq // tm
    nf = ff // tf
    assert nf >= 2
    return pl.pallas_call(
        _ffn_kernel,
        grid=(t // tm, nf),
        in_specs=[pl.BlockSpec((tm, d), lambda i, f: (i, 0)),
                  pl.BlockSpec((d, tf), lambda i, f: (0, f)),
                  pl.BlockSpec((d, tf), lambda i, f: (0, f + nf)),
                  pl.BlockSpec((tf, d), lambda i, f: (f, 0)),
                  pl.BlockSpec(memory_space=pl.ANY),
                  pl.BlockSpec((None, 1, d), lambda i, f: (i // bps, 0, 0)),
                  pl.BlockSpec((1, d), lambda i, f: (0, 0))],
        out_specs=pl.BlockSpec((tm, d), lambda i, f: (i, 0)),
        out_shape=jax.ShapeDtypeStruct((t, d), F32),
        scratch_shapes=[pltpu.VMEM((tm, d), F32), pltpu.SemaphoreType.DMA],
        compiler_params=_params("arbitrary", "arbitrary"),
        name="ffn",
    )(h2, w_in, w_in, w_out, x1, gate, gn)


def _s5_param_layout(lam_re, lam_im, log_step, b_re, b_im, c_re, c_im, d_skip):
    groups = lam_re.shape[1]
    both = lambda p: jnp.concatenate([p[0], p[1]], axis=-1)
    row = lambda p: both(p)[:, None, :]
    ls = jnp.broadcast_to(log_step[:, :, None], log_step.shape + (S5_STATE,))
    bt = lambda p: both(jnp.swapaxes(p, -1, -2))
    eye = jnp.eye(S5_GROUP, dtype=F32)
    dcol = (d_skip.reshape(groups, S5_GROUP, 1) * eye[None]).reshape(groups, S5_GROUP * S5_GROUP, 1)
    return (row(lam_re), row(lam_im), row(ls), bt(b_re), bt(b_im), both(c_re), both(c_im), dcol)


def kernel(x, c, w_ada, b_ada, norm_mix, w_in, s5_lambda_re, s5_lambda_im, s5_log_step,
           s5_b_re, s5_b_im, s5_c_re, s5_c_im, s5_d, w_s5_glu, w_branch_s5, w_branch_fnet,
           w_out, norm_ffn, w_ffn_in, w_ffn_out, norm_final):
    batch, seq, d = x.shape
    t = batch * seq
    w5 = s5_d.shape[-1]
    wf = w_branch_fnet.shape[1]
    n2 = FFT_N2
    n1 = seq // n2
    assert w_in.shape[0] == 1, "multi-layer stacks are not supported"
    assert seq % FFN_ROW_TILE == 0 and n1 % BF16_ROWS == 0

    dft = jnp.asarray(_chan_dft_table()).astype(BF16)
    perm = jnp.asarray(_perm_table(ROW_TILE))
    f1 = jnp.asarray(_fft1_table(n1, seq)).astype(BF16)
    m2 = jnp.asarray(_fft2_table(n1, n2)).astype(BF16)

    xs = x.reshape(t, d).astype(F32)
    ct = c.astype(F32).T
    for l in range(w_in.shape[0]):
        mod = _ada(ct, w_ada[l], b_ada[l])
        sh_m, sc_m, g_m, sh_f, sc_f, g_f = [m.reshape(batch, 1, d)
                                            for m in jnp.split(mod, N_MOD, axis=-1)]
        wl = w_in[l]
        us5, v, gs, gf = _inproj(
            xs, norm_mix[l].reshape(1, d), sc_m, sh_m,
            wl[:, :w5].T.astype(BF16), wl[:, w5:w5 + wf].astype(BF16),
            wl[:, w5 + wf:].astype(BF16), perm, dft, batch, seq)

        s5p = _s5_param_layout(s5_lambda_re[l], s5_lambda_im[l], s5_log_step[l], s5_b_re[l],
                               s5_b_im[l], s5_c_re[l], s5_c_im[l], s5_d[l])
        ys5 = _s5(us5.reshape(t // CHUNK, w5 * CHUNK), *s5p, chunks_per_seq=seq // CHUNK)

        yfn = _fft(f1, m2, v)

        xs, h2 = _merge(
            xs, ys5.reshape(t // CHUNK, w5, CHUNK), yfn.reshape(t, wf), gs, gf, g_m,
            norm_ffn[l].reshape(1, d), sc_f, sh_f, w_s5_glu[l].astype(BF16),
            w_branch_s5[l].astype(BF16), w_branch_fnet[l].astype(BF16), w_out[l].astype(BF16), seq)
        xs = _ffn(h2, w_ffn_in[l].astype(BF16), w_ffn_out[l].astype(BF16), xs, g_f,
                  norm_final.reshape(1, d), seq)
    return xs.reshape(batch, seq, d).astype(x.dtype)
```
